```python
import jax, jax.numpy as jnp
from jax import lax
import numpy as np

D_MODEL = 2048
BATCH = 8
SEQ = 2048
DEPTH = 2

N_EVEN = (DEPTH + 1) // 2
N_ODD = DEPTH // 2
BRANCH = D_MODEL
MIX_WIDTH = 2 * BRANCH

A_HEAD_DIM = 128
A_HEADS = BRANCH // A_HEAD_DIM
A_CHUNK = 32
B_GROUPS = 8
B_GROUP_DIM = BRANCH // B_GROUPS
C_CONV = 3
D_HEAD_DIM = 64
D_HEADS = BRANCH // D_HEAD_DIM
D_GROUPS = 4
D_HPG = D_HEADS // D_GROUPS
D_STATE = 128
D_CONV = 3
D_CHUNK = 64
D_XBC = BRANCH + 2 * D_GROUPS * D_STATE
EPS = 1e-6

EV_SIZES = (BRANCH, BRANCH, BRANCH, BRANCH, BRANCH, BRANCH, BRANCH)
OD_SIZES = (BRANCH, BRANCH, BRANCH, BRANCH, BRANCH, D_XBC, 2 * D_HEADS)
EV_COLS = sum(EV_SIZES)
OD_COLS = sum(OD_SIZES)

kernel_name = "hybrid_hgrn2_fnet_shortconv_ssd_encoder"


def rms_norm(x, w):
    xf = x.astype(jnp.float32)
    y = xf * lax.rsqrt(jnp.mean(xf * xf, axis=-1, keepdims=True) + EPS) * w.astype(jnp.float32)
    return y.astype(x.dtype)


def split_cols(t, sizes):
    idx = np.cumsum(np.array(sizes))[:-1].tolist()
    return jnp.split(t, idx, axis=-1)


def flip(t):
    return t[:, ::-1]


def centred_dwconv(u, w):
    k_w = w.shape[0]
    pad = k_w // 2
    length = u.shape[1]
    up = jnp.pad(u, ((0, 0), (pad, pad), (0, 0)))
    y = up[:, 0:length] * w[0]
    for k in range(1, k_w):
        y = y + up[:, k:k + length] * w[k]
    return y


def gla_chunked(q, k, v, logf):
    bn, length, h, dk = q.shape
    dv = v.shape[-1]
    nc = length // A_CHUNK
    chunks = lambda t: jnp.moveaxis(t.reshape(bn, nc, A_CHUNK, h, t.shape[-1]), 1, 0)
    mask = jnp.tril(jnp.ones((A_CHUNK, A_CHUNK), dtype=bool))

    def step(state, inp):
        qc, kc, vc, gc = inp
        b = jnp.cumsum(gc, axis=1)
        b_last = b[:, -1:]
        q_in = qc * jnp.exp(b)
        k_in = kc * jnp.exp(-b)
        scores = jnp.where(mask, jnp.einsum('bihk,bjhk->bhij', q_in, k_in), 0.0)
        o = (jnp.einsum('bhij,bjhv->bihv', scores, vc)
             + jnp.einsum('bihk,bhkv->bihv', q_in, state))
        k_end = kc * jnp.exp(b_last - b)
        state = (jnp.exp(b_last[:, 0])[..., None] * state
                 + jnp.einsum('bjhk,bjhv->bhkv', k_end, vc))
        return state, o

    s0 = jnp.zeros((bn, h, dk, dv), q.dtype)
    _, o = lax.scan(step, s0, (chunks(q), chunks(k), chunks(v), chunks(logf)))
    return jnp.moveaxis(o, 0, 1).reshape(bn, length, h, dv)


def hgrn2_mixer(a_q, a_i, a_ff, a_fb, lb, norm_w):
    bn, length, _ = a_q.shape
    heads = lambda t: t.astype(jnp.float32).reshape(bn, length, A_HEADS, A_HEAD_DIM)
    q, v = heads(a_q), heads(a_i)
    lb = lb.astype(jnp.float32).reshape(A_HEADS, A_HEAD_DIM)

    def gates(zf):
        z = heads(zf)
        logf = jnp.log(lb + (1.0 - lb) * jax.nn.sigmoid(z))
        key = (1.0 - lb) * jax.nn.sigmoid(-z)
        return key, logf

    k_f, g_f = gates(a_ff)
    k_b, g_b = gates(a_fb)
    o = (gla_chunked(q, k_f, v, g_f)
         + flip(gla_chunked(flip(q), flip(k_b), flip(v), flip(g_b))))
    o = o * lax.rsqrt(jnp.mean(o * o, axis=-1, keepdims=True) + EPS)
    return (o.reshape(bn, length, BRANCH) * norm_w.astype(jnp.float32)).astype(a_q.dtype)


def fourier_mixer(u, fw, fb):
    bn, length, _ = u.shape
    ug = u.astype(jnp.float32).reshape(bn, length, B_GROUPS, B_GROUP_DIM)
    mixed = jnp.fft.fft2(ug, axes=(1, 3), norm="ortho").real
    y = jnp.einsum('blgc,gcd->blgd', mixed, fw.astype(jnp.float32)).reshape(bn, length, BRANCH)
    return (y + fb.astype(jnp.float32)).astype(u.dtype)


def ssd_chunked(x, dt, a, bm, cm):
    bn, length, g, hg, p = x.shape
    n = bm.shape[-1]
    nc = length // D_CHUNK
    chunks = lambda t: jnp.moveaxis(t.reshape(bn, nc, D_CHUNK, *t.shape[2:]), 1, 0)
    mask = jnp.tril(jnp.ones((D_CHUNK, D_CHUNK), dtype=bool))[None, :, :, None, None]

    def step(state, inp):
        xc, dtc, bc, cc = inp
        acum = jnp.cumsum(dtc * a, axis=1)
        decay = jnp.exp(jnp.where(mask, acum[:, :, None] - acum[:, None, :], -jnp.inf))
        xdt = xc * dtc[..., None]
        cb = jnp.einsum('bign,bjgn->bijg', cc, bc)
        y = jnp.einsum('bijg,bijgh,bjghp->bighp', cb, decay, xdt)
        y = y + jnp.einsum('bign,bghpn->bighp', cc, state) * jnp.exp(acum)[..., None]
        to_end = jnp.exp(acum[:, -1:] - acum)
        state = (state * jnp.exp(acum[:, -1])[..., None, None]
                 + jnp.einsum('bjgn,bjgh,bjghp->bghpn', bc, to_end, xdt))
        return state, y

    s0 = jnp.zeros((bn, g, hg, p, n), x.dtype)
    _, y = lax.scan(step, s0, (chunks(x), chunks(dt), chunks(bm), chunks(cm)))
    return jnp.moveaxis(y, 0, 1).reshape(bn, length, g, hg, p)


def ssd_mixer(d_xbc, d_dt, d_z, conv_w, conv_b, dt_bias, a_log, d_skip, norm_w):
    bn, length, _ = d_xbc.shape
    xbc = jax.nn.silu(centred_dwconv(d_xbc, conv_w) + conv_b).astype(jnp.float32)
    xs, bm, cm = split_cols(xbc, (BRANCH, D_GROUPS * D_STATE, D_GROUPS * D_STATE))
    xs = xs.reshape(bn, length, D_GROUPS, D_HPG, D_HEAD_DIM)
    bm = bm.reshape(bn, length, D_GROUPS, D_STATE)
    cm = cm.reshape(bn, length, D_GROUPS, D_STATE)
    dt = jax.nn.softplus(d_dt.astype(jnp.float32).reshape(bn, length, 2, D_HEADS)
                         + dt_bias.astype(jnp.float32))
    dt = dt.reshape(bn, length, 2, D_GROUPS, D_HPG)
    a = (-jnp.exp(a_log.astype(jnp.float32))).reshape(2, D_GROUPS, D_HPG)
    y = (ssd_chunked(xs, dt[:, :, 0], a[0], bm, cm)
         + flip(ssd_chunked(flip(xs), flip(dt[:, :, 1]), a[1], flip(bm), flip(cm))))
    y = y + xs * d_skip.astype(jnp.float32).reshape(D_GROUPS, D_HPG)[..., None]
    y = y.reshape(bn, length, BRANCH) * jax.nn.silu(d_z.astype(jnp.float32))
    return rms_norm(y, norm_w).astype(d_z.dtype)


def even_layer(h, w_in, w_out, lb, hgrn_nw, fw, fb):
    a_q, a_i, a_ff, a_fb, a_g, b_u, b_g = split_cols(h @ w_in, EV_SIZES)
    a_out = hgrn2_mixer(a_q, a_i, a_ff, a_fb, lb, hgrn_nw) * jax.nn.silu(a_g)
    b_out = fourier_mixer(b_u, fw, fb) * jax.nn.silu(b_g)
    return jnp.concatenate([a_out, b_out], axis=-1) @ w_out


def odd_layer(h, w_in, w_out, sconv_w, conv_w, conv_b, dt_bias, a_log, d_skip, ssd_nw):
    c_in, c_b, c_c, c_g, d_z, d_xbc, d_dt = split_cols(h @ w_in, OD_SIZES)
    c_out = c_b * centred_dwconv(c_c * c_in, sconv_w) * jax.nn.silu(c_g)
    d_out = ssd_mixer(d_xbc, d_dt, d_z, conv_w, conv_b, dt_bias, a_log, d_skip, ssd_nw)
    return jnp.concatenate([c_out, d_out], axis=-1) @ w_out


def setup_inputs(seed: int = 0) -> dict:
    key = jax.random.key(seed)
    ks = jax.random.split(key, 18)
    f32 = jnp.float32
    nrm = lambda k, shape, s: jax.random.normal(k, shape, f32) * s
    dt_init = jnp.exp(jax.random.uniform(ks[14], (N_ODD, 2, D_HEADS), f32)
                      * (np.log(0.1) - np.log(0.001)) + np.log(0.001))
    return {
        "x": nrm(ks[0], (BATCH, SEQ, D_MODEL), 1.0),
        "norm_w": 1.0 + nrm(ks[1], (DEPTH, D_MODEL), 0.02),
        "final_norm_w": 1.0 + nrm(ks[2], (D_MODEL,), 0.02),
        "ev_w_in": nrm(ks[3], (N_EVEN, D_MODEL, EV_COLS), D_MODEL ** -0.5),
        "ev_w_out": nrm(ks[4], (N_EVEN, MIX_WIDTH, D_MODEL), MIX_WIDTH ** -0.5),
        "hgrn_lb_logits": nrm(ks[5], (DEPTH + 1, BRANCH), 0.1),
        "hgrn_norm_w": 1.0 + nrm(ks[6], (N_EVEN, BRANCH), 0.02),
        "fnet_w": nrm(ks[7], (N_EVEN, B_GROUPS, B_GROUP_DIM, B_GROUP_DIM), B_GROUP_DIM ** -0.5),
        "fnet_b": nrm(ks[8], (N_EVEN, BRANCH), 0.01),
        "od_w_in": nrm(ks[9], (N_ODD, D_MODEL, OD_COLS), D_MODEL ** -0.5),
        "od_w_out": nrm(ks[10], (N_ODD, MIX_WIDTH, D_MODEL), MIX_WIDTH ** -0.5),
        "sconv_w": nrm(ks[11], (N_ODD, C_CONV, BRANCH), C_CONV ** -0.5),
        "ssd_conv_w": nrm(ks[12], (N_ODD, D_CONV, D_XBC), D_CONV ** -0.5),
        "ssd_conv_b": nrm(ks[13], (N_ODD, D_XBC), 0.01),
        "ssd_dt_bias": dt_init + jnp.log(-jnp.expm1(-dt_init)),
        "ssd_a_log": jnp.log(jax.random.uniform(ks[15], (N_ODD, 2, D_HEADS), f32, 1.0, 16.0)),
        "ssd_d": 1.0 + nrm(ks[16], (N_ODD, D_HEADS), 0.1),
        "ssd_norm_w": 1.0 + nrm(ks[17], (N_ODD, BRANCH), 0.02),
    }


def reference(x, norm_w, final_norm_w, ev_w_in, ev_w_out, hgrn_lb_logits, hgrn_norm_w,
              fnet_w, fnet_b, od_w_in, od_w_out, sconv_w, ssd_conv_w, ssd_conv_b,
              ssd_dt_bias, ssd_a_log, ssd_d, ssd_norm_w):
    lower_bounds = jnp.cumsum(jax.nn.softmax(hgrn_lb_logits.astype(jnp.float32), axis=0), axis=0)
    for layer in range(DEPTH):
        h = rms_norm(x, norm_w[layer])
        if layer % 2 == 0:
            e = layer // 2
            x = x + even_layer(h, ev_w_in[e], ev_w_out[e], lower_bounds[layer],
                               hgrn_norm_w[e], fnet_w[e], fnet_b[e])
        else:
            o = layer // 2
            x = x + odd_layer(h, od_w_in[o], od_w_out[o], sconv_w[o], ssd_conv_w[o],
                              ssd_conv_b[o], ssd_dt_bias[o], ssd_a_log[o], ssd_d[o],
                              ssd_norm_w[o])
    return rms_norm(x, final_norm_w)
```

```python
import functools

import jax
import jax.numpy as jnp
from jax import lax
from jax.experimental import pallas as pl
from jax.experimental.pallas import tpu as pltpu

F32 = jnp.float32
BF16 = jnp.bfloat16
EPS = 1e-6

VMEM_LIMIT_BYTES = 56 * 1024 * 1024
LANES = 128
BF16_SUBLANES = 16

HGRN_HEAD = 128
HGRN_CHUNK = 64
FNET_GROUP = 256
SSD_HEAD = 64
SSD_STATE = 128
SSD_GROUP_HEADS = 8
SSD_GROUP_W = SSD_HEAD * SSD_GROUP_HEADS
SSD_CHUNK = 128


def _params(*sem):
    return pltpu.CompilerParams(dimension_semantics=sem, vmem_limit_bytes=VMEM_LIMIT_BYTES)


def _dot(a, b):
    return jnp.dot(a, b, preferred_element_type=F32)


def _dot_nt(a, b):
    return lax.dot_general(a, b, (((1,), (1,)), ((), ())), preferred_element_type=F32)


def _dot_tn(a, b):
    return lax.dot_general(a, b, (((0,), (0,)), ((), ())), preferred_element_type=F32)


def _split3(x):
    hi = x.astype(BF16)
    r = x - hi.astype(F32)
    mid = r.astype(BF16)
    lo = (r - mid.astype(F32)).astype(BF16)
    return hi, mid, lo


def _dot_exact_rhs(sel, x):
    hi, mid, lo = _split3(x)
    return _dot(sel, hi) + _dot(sel, mid) + _dot(sel, lo)


def _dot_exact_lhs(x, sel):
    hi, mid, lo = _split3(x)
    return _dot(hi, sel) + _dot(mid, sel) + _dot(lo, sel)


def _silu(x):
    return x * jax.nn.sigmoid(x)


def _rms_scale(x):
    return lax.rsqrt(jnp.mean(x * x, axis=-1, keepdims=True) + EPS)


def _norm_rows_to(h_ref, x_ref, nw_ref, rows_per_step):
    nw = nw_ref[...]

    def body(r, carry):
        rows = pl.ds(pl.multiple_of(r * rows_per_step, rows_per_step), rows_per_step)
        x = x_ref[rows, :]
        h_ref[rows, :] = (x * _rms_scale(x) * nw).astype(BF16)
        return carry

    lax.fori_loop(0, x_ref.shape[0] // rows_per_step, body, 0)


def _inproj_kernel(x_ref, nw_ref, w_ref, o_ref, h_ref):
    @pl.when(pl.program_id(1) == 0)
    def _():
        _norm_rows_to(h_ref, x_ref, nw_ref, 128)

    o_ref[...] = _dot(h_ref[...], w_ref[...]).astype(o_ref.dtype)


def _inproj_dt_kernel(x_ref, nw_ref, w_ref, wdt_ref, o_ref, dt_ref, h_ref):
    @pl.when(pl.program_id(1) == 0)
    def _():
        _norm_rows_to(h_ref, x_ref, nw_ref, 128)
        dt_ref[...] = _dot(h_ref[...], wdt_ref[...])

    o_ref[...] = _dot(h_ref[...], w_ref[...]).astype(o_ref.dtype)


def _inproj(x2, nw, w, wdt=None, *, tm, tn):
    t, d = x2.shape
    n = w.shape[1]
    tm = min(tm, t)
    grid = (t // tm, n // tn)
    in_specs = [
        pl.BlockSpec((tm, d), lambda i, j: (i, 0)),
        pl.BlockSpec((1, d), lambda i, j: (0, 0)),
        pl.BlockSpec((d, tn), lambda i, j: (0, j)),
    ]
    out_main = pl.BlockSpec((tm, tn), lambda i, j: (i, j))
    scratch = [pltpu.VMEM((tm, d), BF16)]
    if wdt is None:
        return pl.pallas_call(
            _inproj_kernel, grid=grid, in_specs=in_specs, out_specs=out_main,
            out_shape=jax.ShapeDtypeStruct((t, n), BF16), scratch_shapes=scratch,
            compiler_params=_params("parallel", "arbitrary"), name="inproj",
        )(x2, nw, w)
    ndt = wdt.shape[1]
    return pl.pallas_call(
        _inproj_dt_kernel, grid=grid,
        in_specs=in_specs + [pl.BlockSpec((d, ndt), lambda i, j: (0, 0))],
        out_specs=[out_main, pl.BlockSpec((tm, ndt), lambda i, j: (i, 0))],
        out_shape=[jax.ShapeDtypeStruct((t, n), BF16), jax.ShapeDtypeStruct((t, ndt), F32)],
        scratch_shapes=scratch,
        compiler_params=_params("parallel", "arbitrary"), name="inproj_dt",
    )(x2, nw, w, wdt)


def _outproj_kernel(*refs, norm_b, final_norm):
    a_ref, b_ref, x_ref, w_ref = refs[:4]
    rest = list(refs[4:])
    o_ref = rest.pop()
    ka = a_ref.shape[1]
    b = b_ref[...]
    if norm_b:
        bnw_ref = rest.pop(0)
        bf = b.astype(F32)
        b = (bf * _rms_scale(bf) * bnw_ref[...]).astype(BF16)
    acc = _dot(a_ref[...], w_ref[0:ka, :]) + _dot(b, w_ref[ka:, :]) + x_ref[...]
    if final_norm:
        fnw_ref = rest.pop(0)
        acc = acc * _rms_scale(acc) * fnw_ref[...]
    o_ref[...] = acc


def _outproj(a, b, x2, w, b_norm_w=None, final_norm_w=None, *, tm):
    t, d = x2.shape
    ka, kb = a.shape[1], b.shape[1]
    tm = min(tm, t)
    row = lambda width: pl.BlockSpec((tm, width), lambda i: (i, 0))
    const = lambda shape: pl.BlockSpec(shape, lambda i: (0, 0))
    in_specs = [row(ka), row(kb), row(d), const((ka + kb, d))]
    args = [a, b, x2, w]
    if b_norm_w is not None:
        in_specs.append(const((1, kb)))
        args.append(b_norm_w)
    if final_norm_w is not None:
        in_specs.append(const((1, d)))
        args.append(final_norm_w)
    kern = functools.partial(_outproj_kernel, norm_b=b_norm_w is not None,
                             final_norm=final_norm_w is not None)
    return pl.pallas_call(
        kern, grid=(t // tm,), in_specs=in_specs, out_specs=row(d),
        out_shape=jax.ShapeDtypeStruct((t, d), F32),
        compiler_params=_params("parallel"), name="outproj",
    )(*args)


def _hgrn_kernel(q_ref, v_ref, zf_ref, zb_ref, g_ref, lb_ref, nw_ref, o_ref, acc_ref, *, chunk):
    length, width = acc_ref.shape
    n_chunks = length // chunk
    lb = lb_ref[...]
    one_m_lb = 1.0 - lb
    nw = nw_ref[...]
    ri = lax.broadcasted_iota(jnp.int32, (chunk, chunk), 0)
    ci = lax.broadcasted_iota(jnp.int32, (chunk, chunk), 1)
    mid = chunk // 2

    for backward in (False, True):
        mask = (ri <= ci) if backward else (ri >= ci)
        tri = jnp.where(mask, 1.0, 0.0).astype(BF16)
        last = 0 if backward else chunk - 1
        z_ref = zb_ref if backward else zf_ref

        def body(c, state_t, backward=backward, mask=mask, tri=tri, last=last, z_ref=z_ref):
            cc = (n_chunks - 1 - c) if backward else c
            rows = pl.ds(pl.multiple_of(cc * chunk, chunk), chunk)
            z = z_ref[rows, :].astype(F32)
            sg = jax.nn.sigmoid(z)
            logf = jnp.log(lb + one_m_lb * sg)
            key = one_m_lb * (1.0 - sg)
            b = _dot_exact_rhs(tri, logf)
            b_mid = b[mid:mid + 1, :]
            b_last = b[last:last + 1, :]
            eq = jnp.exp(b - b_mid)
            ek = jnp.exp(b_mid - b)
            q = q_ref[rows, :].astype(F32)
            v = v_ref[rows, :]
            q_in = q * eq
            k_in = key * ek
            s = _dot_nt(q_in.astype(BF16), k_in.astype(BF16))
            s = jnp.where(mask, s, 0.0).astype(BF16)
            q_dec = (q_in * jnp.exp(b_mid)).astype(BF16)
            o = _dot(s, v) + _dot_nt(q_dec, state_t.astype(BF16))
            k_end = (k_in * jnp.exp(b_last - b_mid)).astype(BF16)
            new_state = state_t * jnp.exp(b_last) + _dot_tn(v, k_end)
            if backward:
                tot = acc_ref[rows, :] + o
                g = g_ref[rows, :].astype(F32)
                o_ref[rows, :] = (tot * _rms_scale(tot) * nw * _silu(g)).astype(o_ref.dtype)
            else:
                acc_ref[rows, :] = o
            return new_state

        lax.fori_loop(0, n_chunks, body, jnp.zeros((width, width), F32))


def _hgrn(p3, lb, nw, *, col0, chunk):
    bsz, length, _ = p3.shape
    width = HGRN_HEAD
    branch = lb.shape[1]
    heads = branch // width
    c0 = col0 // width
    seq = lambda k: pl.BlockSpec((None, length, width), lambda b, h, k=k: (b, 0, c0 + k * heads + h))
    vec = pl.BlockSpec((1, width), lambda b, h: (0, h))
    return pl.pallas_call(
        functools.partial(_hgrn_kernel, chunk=chunk),
        grid=(bsz, heads),
        in_specs=[seq(0), seq(1), seq(2), seq(3), seq(4), vec, vec],
        out_specs=pl.BlockSpec((None, length, width), lambda b, h: (b, 0, h)),
        out_shape=jax.ShapeDtypeStruct((bsz, length, branch), BF16),
        scratch_shapes=[pltpu.VMEM((length, width), F32)],
        compiler_params=_params("parallel", "parallel"), name="hgrn2",
    )(p3, p3, p3, p3, p3, lb, nw)


def _dft_cos_sin(n):
    k = lax.broadcasted_iota(jnp.int32, (n, n), 0) * lax.broadcasted_iota(jnp.int32, (n, n), 1)
    ang = (k % n).astype(F32) * (2.0 * jnp.pi / n)
    return jnp.cos(ang).astype(BF16), jnp.sin(ang).astype(BF16)


def _fnet_cols_kernel(u_ref, cc_ref, sc_ref, zc_ref, zs_ref, *, group):
    for g in range(u_ref.shape[1] // group):
        cols = slice(g * group, (g + 1) * group)
        u = u_ref[:, cols]
        zc_ref[:, cols] = _dot(u, cc_ref[...]).astype(BF16)
        zs_ref[:, cols] = _dot(u, sc_ref[...]).astype(BF16)


def _fnet_rows_kernel(cl_ref, sl_ref, zc_ref, zs_ref, fw_ref, fb_ref, g_ref, o_ref, *, scale):
    mixed = (_dot(cl_ref[...], zc_ref[...]) - _dot(sl_ref[...], zs_ref[...])) * scale
    y = _dot(mixed.astype(BF16), fw_ref[...]) + fb_ref[...]
    o_ref[...] = (y * _silu(g_ref[...].astype(F32))).astype(o_ref.dtype)


def _fnet(p2, p3, fw, fb, *, u_col0, g_col0, tm):
    t = p2.shape[0]
    bsz, length, _ = p3.shape
    groups, gd, _ = fw.shape
    branch = groups * gd
    tm = min(tm, t)
    cc, sc = _dft_cos_sin(gd)
    cl, sl = _dft_cos_sin(length)
    const2 = lambda shape: pl.BlockSpec(shape, lambda i: (0, 0))
    zc, zs = pl.pallas_call(
        functools.partial(_fnet_cols_kernel, group=gd),
        grid=(t // tm,),
        in_specs=[pl.BlockSpec((tm, branch), lambda i: (i, u_col0 // branch)),
                  const2((gd, gd)), const2((gd, gd))],
        out_specs=[pl.BlockSpec((tm, branch), lambda i: (i, 0))] * 2,
        out_shape=[jax.ShapeDtypeStruct((t, branch), BF16)] * 2,
        compiler_params=_params("parallel"), name="fnet_cols",
    )(p2, cc, sc)
    zc = zc.reshape(bsz, length, branch)
    zs = zs.reshape(bsz, length, branch)
    gc0 = g_col0 // gd
    seq = lambda c0: pl.BlockSpec((None, length, gd), lambda b, g: (b, 0, c0 + g))
    const = pl.BlockSpec((length, length), lambda b, g: (0, 0))
    return pl.pallas_call(
        functools.partial(_fnet_rows_kernel, scale=float((length * gd) ** -0.5)),
        grid=(bsz, groups),
        in_specs=[const, const, seq(0), seq(0),
                  pl.BlockSpec((None, gd, gd), lambda b, g: (g, 0, 0)),
                  pl.BlockSpec((1, gd), lambda b, g: (0, g)),
                  seq(gc0)],
        out_specs=seq(0),
        out_shape=jax.ShapeDtypeStruct((bsz, length, branch), BF16),
        compiler_params=_params("parallel", "parallel"), name="fnet_rows",
    )(cl, sl, zc, zs, fw, fb, p3)


def _conv3_rows(load_rows, r0, n_rows, length, w_ref):
    h = BF16_SUBLANES
    u = load_rows(r0, n_rows)
    prev_blk = load_rows(pl.multiple_of(jnp.maximum(r0 - h, 0), h), h)
    next_blk = load_rows(pl.multiple_of(jnp.minimum(r0 + n_rows, length - h), h), h)
    prev_row = jnp.where(r0 > 0, prev_blk[h - 1:h, :], 0.0)
    next_row = jnp.where(r0 + n_rows < length, next_blk[0:1, :], 0.0)
    ri = lax.broadcasted_iota(jnp.int32, u.shape, 0)
    up = jnp.where(ri == 0, prev_row, pltpu.roll(u, 1, 0))
    un = jnp.where(ri == n_rows - 1, next_row, pltpu.roll(u, n_rows - 1, 0))
    return up * w_ref[0:1, :] + u * w_ref[1:2, :] + un * w_ref[2:3, :]


def _sconv_kernel(cin_ref, cb_ref, cc_ref, cg_ref, w_ref, o_ref, *, rows_per_step):
    length = o_ref.shape[0]

    def load_u(start, size):
        rows = pl.ds(start, size)
        return cc_ref[rows, :].astype(F32) * cin_ref[rows, :].astype(F32)

    def body(r, carry):
        r0 = pl.multiple_of(r * rows_per_step, rows_per_step)
        rows = pl.ds(r0, rows_per_step)
        conv = _conv3_rows(load_u, r0, rows_per_step, length, w_ref)
        o_ref[rows, :] = (cb_ref[rows, :].astype(F32) * conv
                          * _silu(cg_ref[rows, :].astype(F32))).astype(o_ref.dtype)
        return carry

    lax.fori_loop(0, length // rows_per_step, body, 0)


def _sconv(p3, w, *, tn):
    bsz, length, _ = p3.shape
    branch = w.shape[1]
    nb = branch // tn
    seq = lambda k: pl.BlockSpec((None, length, tn), lambda b, j, k=k: (b, 0, k * nb + j))
    return pl.pallas_call(
        functools.partial(_sconv_kernel, rows_per_step=min(256, length)),
        grid=(bsz, nb),
        in_specs=[seq(0), seq(1), seq(2), seq(3), pl.BlockSpec((3, tn), lambda b, j: (0, j))],
        out_specs=seq(0),
        out_shape=jax.ShapeDtypeStruct((bsz, length, branch), BF16),
        compiler_params=_params("parallel", "parallel"), name="sconv",
    )(p3, p3, p3, p3, w)


def _ssd_kernel(x_ref, bm_ref, cm_ref, z_ref, dt_ref,
                wx_ref, wb_ref, wc_ref, bx_ref, bb_ref, bc_ref,
                dtb_ref, a_ref, aexp_ref, dsk_ref,
                o_ref,
                xs_s, bmt_s, cm_s, dt_s, y_s, st_s, *, chunk, rows_per_step):
    length, gw = xs_s.shape
    n_chunks = length // chunk
    heads = SSD_GROUP_HEADS
    hd = gw // heads

    def conv_silu(ref, w_ref, b_ref, r0):
        load = lambda start, size: ref[pl.ds(start, size), :].astype(F32)
        return _silu(_conv3_rows(load, r0, rows_per_step, length, w_ref) + b_ref[...])

    def pre(r, carry):
        r0 = pl.multiple_of(r * rows_per_step, rows_per_step)
        rows = pl.ds(r0, rows_per_step)
        xs_s[rows, :] = conv_silu(x_ref, wx_ref, bx_ref, r0)
        cm_s[rows, :] = conv_silu(cm_ref, wc_ref, bc_ref, r0).astype(BF16)
        bmv = conv_silu(bm_ref, wb_ref, bb_ref, r0)
        for k in range(rows_per_step // chunk):
            bmt_s[r * (rows_per_step // chunk) + k] = bmv[k * chunk:(k + 1) * chunk, :].T.astype(BF16)
        t = dt_ref[rows, :] + dtb_ref[...]
        dt_s[rows, :] = jnp.maximum(t, 0.0) + jnp.log1p(jnp.exp(-jnp.abs(t)))
        return carry

    lax.fori_loop(0, length // rows_per_step, pre, 0)

    ri = lax.broadcasted_iota(jnp.int32, (chunk, chunk), 0)
    ci = lax.broadcasted_iota(jnp.int32, (chunk, chunk), 1)
    lane = lax.broadcasted_iota(jnp.int32, (chunk, LANES), 1)
    sel_l = lax.broadcasted_iota(jnp.int32, (LANES, gw), 0)
    sel_c = lax.broadcasted_iota(jnp.int32, (LANES, gw), 1) // hd
    dsk = dsk_ref[...]

    for backward in (False, True):
        d = 1 if backward else 0
        mask = (ri <= ci) if backward else (ri >= ci)
        tri = jnp.where(mask, 1.0, 0.0).astype(BF16)
        last = 0 if backward else chunk - 1
        expand = jnp.where(sel_l == sel_c + d * heads, 1.0, 0.0).astype(BF16)
        a_row = a_ref[...]
        a_exp = aexp_ref[d:d + 1, :]
        st_s[...] = jnp.zeros_like(st_s)

        def body(c, carry, d=d, backward=backward, mask=mask, tri=tri, last=last,
                 expand=expand, a_row=a_row, a_exp=a_exp):
            cc = (n_chunks - 1 - c) if backward else c
            rows = pl.ds(pl.multiple_of(cc * chunk, chunk), chunk)
            dtc = dt_s[rows, :]
            acum_c = _dot_exact_rhs(tri, dtc * a_row)
            acum = _dot_exact_lhs(acum_c, expand)
            dte = _dot_exact_lhs(dtc, expand)
            acum_t = acum_c.T
            xs_c = xs_s[rows, :]
            xdt = xs_c * dte
            cm_c = cm_s[rows, :]
            bmt_c = bmt_s[cc]
            cb = _dot(cm_c, bmt_c)
            state = st_s[...]
            y = _dot(cm_c, state.astype(BF16)) * jnp.exp(acum)
            a_last = acum[last:last + 1, :]
            to_end = jnp.exp(a_last - acum)
            st_s[...] = state * jnp.exp(a_last) + _dot(bmt_c, (xdt * to_end).astype(BF16))
            pairs = []
            for p in range(heads // 2):
                ms = []
                for hh in (2 * p, 2 * p + 1):
                    ln = d * heads + hh
                    diff = acum_c[:, ln:ln + 1] - acum_t[ln:ln + 1, :]
                    decay = jnp.exp(jnp.where(mask, diff, -1e30))
                    ms.append((cb * decay).astype(BF16))
                xp = xdt[:, p * LANES:(p + 1) * LANES]
                top = jnp.where(lane < hd, xp, 0.0).astype(BF16)
                bot = jnp.where(lane >= hd, xp, 0.0).astype(BF16)
                pairs.append(_dot(jnp.concatenate(ms, axis=1), jnp.concatenate([top, bot], axis=0)))
            y = y + jnp.concatenate(pairs, axis=1)
            if backward:
                tot = y_s[rows, :] + y + xs_c * dsk
                o_ref[rows, :] = (tot * _silu(z_ref[rows, :].astype(F32))).astype(o_ref.dtype)
            else:
                y_s[rows, :] = y
            return carry

        lax.fori_loop(0, n_chunks, body, 0)


def _ssd(p3, dt3, conv_w, conv_b, dt_bias_g, a_g, a_exp, dsk, *, x_col0, z_col0, chunk):
    bsz, length, _ = p3.shape
    groups = dt3.shape[2] // LANES
    gw, ns = SSD_GROUP_W, SSD_STATE
    branch = groups * gw
    xb, bb, cb = x_col0 // gw, (x_col0 + branch) // ns, (x_col0 + branch + groups * ns) // ns
    seq = lambda w, c0: pl.BlockSpec((None, length, w), lambda b, g: (b, 0, c0 + g))
    par = lambda rows, w, c0: pl.BlockSpec((rows, w), lambda b, g: (0, c0 + g))
    n_chunks = length // chunk
    rows_per_step = min(256, length)
    return pl.pallas_call(
        functools.partial(_ssd_kernel, chunk=chunk, rows_per_step=rows_per_step),
        grid=(bsz, groups),
        in_specs=[seq(gw, xb), seq(ns, bb), seq(ns, cb), seq(gw, z_col0 // gw), seq(LANES, 0),
                  par(3, gw, 0), par(3, ns, branch // ns), par(3, ns, (branch + groups * ns) // ns),
                  par(1, gw, 0), par(1, ns, branch // ns), par(1, ns, (branch + groups * ns) // ns),
                  par(1, LANES, 0), par(1, LANES, 0),
                  pl.BlockSpec((None, 2, gw), lambda b, g: (g, 0, 0)),
                  par(1, gw, 0)],
        out_specs=seq(gw, 0),
        out_shape=jax.ShapeDtypeStruct((bsz, length, branch), BF16),
        scratch_shapes=[pltpu.VMEM((length, gw), F32),
                        pltpu.VMEM((n_chunks, ns, chunk), BF16),
                        pltpu.VMEM((length, ns), BF16),
                        pltpu.VMEM((length, LANES), F32),
                        pltpu.VMEM((length, gw), F32),
                        pltpu.VMEM((ns, gw), F32)],
        compiler_params=_params("parallel", "parallel"), name="ssd",
    )(p3, p3, p3, p3, dt3, conv_w, conv_w, conv_w, conv_b, conv_b, conv_b,
      dt_bias_g, a_g, a_exp, dsk)


def _ssd_group_lanes(v, groups):
    hpg = v.shape[1] // groups
    per = jnp.concatenate([v[0].reshape(groups, hpg), v[1].reshape(groups, hpg)], axis=1)
    per = jnp.pad(per, ((0, 0), (0, LANES - 2 * hpg)))
    return per.reshape(1, groups * LANES)


def kernel(x, norm_w, final_norm_w, ev_w_in, ev_w_out, hgrn_lb_logits, hgrn_norm_w, fnet_w, fnet_b,
           od_w_in, od_w_out, sconv_w, ssd_conv_w, ssd_conv_b, ssd_dt_bias, ssd_a_log, ssd_d, ssd_norm_w):
    bsz, length, d = x.shape
    t = bsz * length
    branch = d
    x2 = x.reshape(t, d)
    row = lambda v: v.astype(F32).reshape(1, -1)

    lower_bounds = jnp.cumsum(jax.nn.softmax(hgrn_lb_logits.astype(F32), axis=0), axis=0)
    p = _inproj(x2, row(norm_w[0]), ev_w_in[0].astype(BF16), tm=1024, tn=1024)
    p3 = p.reshape(bsz, length, -1)
    a_out = _hgrn(p3, row(lower_bounds[0]), row(hgrn_norm_w[0]), col0=0, chunk=HGRN_CHUNK)
    b_out = _fnet(p, p3, fnet_w[0].astype(BF16), row(fnet_b[0]),
                  u_col0=5 * branch, g_col0=6 * branch, tm=512)
    x2 = _outproj(a_out.reshape(t, branch), b_out.reshape(t, branch), x2,
                  ev_w_out[0].astype(BF16), tm=512)

    heads = ssd_d.shape[1]
    groups = heads // SSD_GROUP_HEADS
    n_main = 5 * branch + branch + 2 * groups * SSD_STATE
    w_in = od_w_in[0]
    w_dt = w_in[:, n_main:].reshape(d, 2, groups, SSD_GROUP_HEADS)
    w_dt = jnp.transpose(w_dt, (0, 2, 1, 3)).reshape(d, groups, 2 * SSD_GROUP_HEADS)
    w_dt = jnp.pad(w_dt, ((0, 0), (0, 0), (0, LANES - 2 * SSD_GROUP_HEADS))).reshape(d, groups * LANES)
    p, dt = _inproj(x2, row(norm_w[1]), w_in[:, :n_main].astype(BF16), w_dt.astype(BF16),
                    tm=1024, tn=1024)
    p3 = p.reshape(bsz, length, n_main)
    c_out = _sconv(p3, sconv_w[0].astype(F32), tn=512)
    a_neg = -jnp.exp(ssd_a_log[0].astype(F32))
    a_exp = jnp.repeat(a_neg.reshape(2, groups, SSD_GROUP_HEADS), SSD_HEAD, axis=2)
    d_out = _ssd(p3, dt.reshape(bsz, length, groups * LANES),
                 ssd_conv_w[0].astype(F32), row(ssd_conv_b[0]),
                 _ssd_group_lanes(ssd_dt_bias[0].astype(F32), groups),
                 _ssd_group_lanes(a_neg, groups),
                 jnp.transpose(a_exp, (1, 0, 2)),
                 row(jnp.repeat(ssd_d[0].astype(F32), SSD_HEAD)),
                 x_col0=5 * branch, z_col0=4 * branch, chunk=SSD_CHUNK)
    out = _outproj(c_out.reshape(t, branch), d_out.reshape(t, branch), x2,
                   od_w_out[0].astype(BF16), row(ssd_norm_w[0]), row(final_norm_w), tm=512)
    return out.reshape(bsz, length, d)
```

```python
import functools

import jax
import jax.numpy as jnp
from jax import lax
from jax.experimental import pallas as pl
from jax.experimental.pallas import tpu as pltpu

F32 = jnp.float32
BF16 = jnp.bfloat16
EPS = 1e-6

VMEM_LIMIT_BYTES = 56 * 1024 * 1024
LANES = 128
BF16_SUBLANES = 16

HGRN_HEAD = 128
HGRN_CHUNK = 64
FNET_GROUP = 256
SSD_HEAD = 64
SSD_STATE = 128
SSD_GROUP_HEADS = 8
SSD_GROUP_W = SSD_HEAD * SSD_GROUP_HEADS
SSD_CHUNK = 128


def _params(*sem):
    return pltpu.CompilerParams(dimension_semantics=sem, vmem_limit_bytes=VMEM_LIMIT_BYTES)


def _dot(a, b):
    return jnp.dot(a, b, preferred_element_type=F32)


def _dot_nt(a, b):
    return lax.dot_general(a, b, (((1,), (1,)), ((), ())), preferred_element_type=F32)


def _dot_tn(a, b):
    return lax.dot_general(a, b, (((0,), (0,)), ((), ())), preferred_element_type=F32)


def _split2(x):
    hi = x.astype(BF16)
    lo = (x - hi.astype(F32)).astype(BF16)
    return hi, lo


def _sel_dot_rhs(sel_sel, x):
    return _dot(sel_sel, jnp.concatenate(_split2(x), axis=0))


def _sel_dot_lhs(x, sel_sel):
    return _dot(jnp.concatenate(_split2(x), axis=1), sel_sel)


def _silu(x):
    return x * jax.nn.sigmoid(x)


def _rms_scale(x):
    return lax.rsqrt(jnp.mean(x * x, axis=-1, keepdims=True) + EPS)


def _norm_rows_to(h_ref, x_ref, nw_ref, rows_per_step):
    nw = nw_ref[...]

    def body(r, carry):
        rows = pl.ds(pl.multiple_of(r * rows_per_step, rows_per_step), rows_per_step)
        x = x_ref[rows, :]
        h_ref[rows, :] = (x * _rms_scale(x) * nw).astype(BF16)
        return carry

    lax.fori_loop(0, x_ref.shape[0] // rows_per_step, body, 0)


def _inproj_kernel(x_ref, nw_ref, w_ref, o_ref, h_ref):
    @pl.when(pl.program_id(1) == 0)
    def _():
        _norm_rows_to(h_ref, x_ref, nw_ref, 128)

    o_ref[...] = _dot(h_ref[...], w_ref[...]).astype(o_ref.dtype)


def _inproj_dt_kernel(x_ref, nw_ref, w_ref, wdt_ref, o_ref, dt_ref, h_ref):
    @pl.when(pl.program_id(1) == 0)
    def _():
        _norm_rows_to(h_ref, x_ref, nw_ref, 128)
        dt_ref[...] = _dot(h_ref[...], wdt_ref[...])

    o_ref[...] = _dot(h_ref[...], w_ref[...]).astype(o_ref.dtype)


def _inproj(x2, nw, w, wdt=None, *, tm, tn):
    t, d = x2.shape
    n = w.shape[1]
    tm = min(tm, t)
    grid = (t // tm, n // tn)
    in_specs = [
        pl.BlockSpec((tm, d), lambda i, j: (i, 0)),
        pl.BlockSpec((1, d), lambda i, j: (0, 0)),
        pl.BlockSpec((d, tn), lambda i, j: (0, j)),
    ]
    out_main = pl.BlockSpec((tm, tn), lambda i, j: (i, j))
    scratch = [pltpu.VMEM((tm, d), BF16)]
    if wdt is None:
        return pl.pallas_call(
            _inproj_kernel, grid=grid, in_specs=in_specs, out_specs=out_main,
            out_shape=jax.ShapeDtypeStruct((t, n), BF16), scratch_shapes=scratch,
            compiler_params=_params("parallel", "arbitrary"), name="inproj",
        )(x2, nw, w)
    ndt = wdt.shape[1]
    return pl.pallas_call(
        _inproj_dt_kernel, grid=grid,
        in_specs=in_specs + [pl.BlockSpec((d, ndt), lambda i, j: (0, 0))],
        out_specs=[out_main, pl.BlockSpec((tm, ndt), lambda i, j: (i, 0))],
        out_shape=[jax.ShapeDtypeStruct((t, n), BF16), jax.ShapeDtypeStruct((t, ndt), F32)],
        scratch_shapes=scratch,
        compiler_params=_params("parallel", "arbitrary"), name="inproj_dt",
    )(x2, nw, w, wdt)


def _outproj_kernel(*refs, norm_b, final_norm):
    a_ref, b_ref, x_ref, w_ref = refs[:4]
    rest = list(refs[4:])
    o_ref = rest.pop()
    ka = a_ref.shape[1]
    b = b_ref[...]
    if norm_b:
        bnw_ref = rest.pop(0)
        bf = b.astype(F32)
        b = (bf * _rms_scale(bf) * bnw_ref[...]).astype(BF16)
    acc = _dot(a_ref[...], w_ref[0:ka, :]) + _dot(b, w_ref[ka:, :]) + x_ref[...]
    if final_norm:
        fnw_ref = rest.pop(0)
        acc = acc * _rms_scale(acc) * fnw_ref[...]
    o_ref[...] = acc


def _outproj(a, b, x2, w, b_norm_w=None, final_norm_w=None, *, tm):
    t, d = x2.shape
    ka, kb = a.shape[1], b.shape[1]
    tm = min(tm, t)
    row = lambda width: pl.BlockSpec((tm, width), lambda i: (i, 0))
    const = lambda shape: pl.BlockSpec(shape, lambda i: (0, 0))
    in_specs = [row(ka), row(kb), row(d), const((ka + kb, d))]
    args = [a, b, x2, w]
    if b_norm_w is not None:
        in_specs.append(const((1, kb)))
        args.append(b_norm_w)
    if final_norm_w is not None:
        in_specs.append(const((1, d)))
        args.append(final_norm_w)
    kern = functools.partial(_outproj_kernel, norm_b=b_norm_w is not None,
                             final_norm=final_norm_w is not None)
    return pl.pallas_call(
        kern, grid=(t // tm,), in_specs=in_specs, out_specs=row(d),
        out_shape=jax.ShapeDtypeStruct((t, d), F32),
        compiler_params=_params("parallel"), name="outproj",
    )(*args)


def _hgrn_kernel(q_ref, v_ref, zf_ref, zb_ref, g_ref, lb_ref, nw_ref, o_ref,
                 qin_s, kin_s, qdec_s, kend_s, vt_s, dec_s, acc_s, *, chunk, block):
    length, width = acc_s.shape
    n_chunks = length // chunk
    per_block = block // chunk
    lb = lb_ref[...]
    one_m_lb = 1.0 - lb
    nw = nw_ref[...]
    mid = chunk // 2
    z_refs = (zf_ref, zb_ref)
    edge = (chunk - 1, 0)

    bri = lax.broadcasted_iota(jnp.int32, (block, block), 0)
    bci = lax.broadcasted_iota(jnp.int32, (block, block), 1)
    same_chunk = (bri // chunk) == (bci // chunk)
    tri = tuple(jnp.where(same_chunk & m, 1.0, 0.0).astype(BF16) for m in (bri >= bci, bri <= bci))

    def prepare(blk, carry):
        r0 = pl.multiple_of(blk * block, block)
        rows = pl.ds(r0, block)
        q = q_ref[rows, :].astype(F32)
        vf = v_ref[rows, :].astype(F32)
        for k in range(per_block):
            vt_s[blk * per_block + k] = vf[k * chunk:(k + 1) * chunk, :].T.astype(BF16)
        for d in (0, 1):
            z = z_refs[d][rows, :].astype(F32)
            sg = jax.nn.sigmoid(z)
            logf = jnp.log(lb + one_m_lb * sg)
            key = one_m_lb * (1.0 - sg)
            b2 = _dot(tri[d], jnp.concatenate(_split2(logf), axis=1))
            b = b2[:, :width] + b2[:, width:]
            for k in range(per_block):
                sl = slice(k * chunk, (k + 1) * chunk)
                rk = pl.ds(r0 + k * chunk, chunk)
                bk = b[sl]
                b_mid = bk[mid:mid + 1, :]
                b_edge = bk[edge[d]:edge[d] + 1, :]
                q_in = q[sl] * jnp.exp(bk - b_mid)
                k_in = key[sl] * jnp.exp(b_mid - bk)
                qin_s[d, rk, :] = q_in.astype(BF16)
                kin_s[d, rk, :] = k_in.astype(BF16)
                qdec_s[d, rk, :] = (q_in * jnp.exp(b_mid)).astype(BF16)
                kend_s[d, rk, :] = (k_in * jnp.exp(b_edge - b_mid)).astype(BF16)
                dec_s[d * n_chunks + blk * per_block + k] = jnp.broadcast_to(jnp.exp(b_edge), (8, width))
        return carry

    lax.fori_loop(0, length // block, prepare, 0, unroll=2)

    ri = lax.broadcasted_iota(jnp.int32, (chunk, chunk), 0)
    ci = lax.broadcasted_iota(jnp.int32, (chunk, chunk), 1)
    masks = (ri >= ci, ri <= ci)

    def scan(c, states, *, finalize):
        new_states = []
        for d in (0, 1):
            cc = c if d == 0 else n_chunks - 1 - c
            rows = pl.ds(pl.multiple_of(cc * chunk, chunk), chunk)
            s = _dot_nt(qin_s[d, rows, :], kin_s[d, rows, :])
            s = jnp.where(masks[d], s, 0.0).astype(BF16)
            o = _dot(s, v_ref[rows, :]) + _dot_nt(qdec_s[d, rows, :], states[d].astype(BF16))
            new_states.append(states[d] * dec_s[d * n_chunks + cc][0:1, :] + _dot(vt_s[cc], kend_s[d, rows, :]))
            if finalize:
                tot = acc_s[rows, :] + o
                g = g_ref[rows, :].astype(F32)
                o_ref[rows, :] = (tot * _rms_scale(tot) * nw * _silu(g)).astype(o_ref.dtype)
            else:
                acc_s[rows, :] = o
        return tuple(new_states)

    zero = jnp.zeros((width, width), F32)
    half = n_chunks // 2
    states = lax.fori_loop(0, half, functools.partial(scan, finalize=False), (zero, zero), unroll=4)
    lax.fori_loop(half, n_chunks, functools.partial(scan, finalize=True), states, unroll=4)


def _hgrn(p3, lb, nw, *, col0, chunk):
    bsz, length, _ = p3.shape
    width = HGRN_HEAD
    branch = lb.shape[1]
    heads = branch // width
    c0 = col0 // width
    n_chunks = length // chunk
    seq = lambda k: pl.BlockSpec((None, length, width), lambda b, h, k=k: (b, 0, c0 + k * heads + h))
    vec = pl.BlockSpec((1, width), lambda b, h: (0, h))
    scaled = pltpu.VMEM((2, length, width), BF16)
    return pl.pallas_call(
        functools.partial(_hgrn_kernel, chunk=chunk, block=min(128, length)),
        grid=(bsz, heads),
        in_specs=[seq(0), seq(1), seq(2), seq(3), seq(4), vec, vec],
        out_specs=pl.BlockSpec((None, length, width), lambda b, h: (b, 0, h)),
        out_shape=jax.ShapeDtypeStruct((bsz, length, branch), BF16),
        scratch_shapes=[scaled, scaled, scaled, scaled,
                        pltpu.VMEM((n_chunks, width, chunk), BF16),
                        pltpu.VMEM((2 * n_chunks, 8, width), F32),
                        pltpu.VMEM((length, width), F32)],
        compiler_params=_params("parallel", "parallel"), name="hgrn2",
    )(p3, p3, p3, p3, p3, lb, nw)


def _dft_cos_sin(n):
    k = lax.broadcasted_iota(jnp.int32, (n, n), 0) * lax.broadcasted_iota(jnp.int32, (n, n), 1)
    ang = (k % n).astype(F32) * (2.0 * jnp.pi / n)
    return jnp.cos(ang).astype(BF16), jnp.sin(ang).astype(BF16)


def _fnet_cols_kernel(u_ref, cc_ref, sc_ref, zc_ref, zs_ref, *, group):
    for g in range(u_ref.shape[1] // group):
        cols = slice(g * group, (g + 1) * group)
        u = u_ref[:, cols]
        zc_ref[:, cols] = _dot(u, cc_ref[...]).astype(BF16)
        zs_ref[:, cols] = _dot(u, sc_ref[...]).astype(BF16)


def _fnet_rows_kernel(cl_ref, sl_ref, zc_ref, zs_ref, fw_ref, fb_ref, g_ref, o_ref, *, scale):
    mixed = (_dot(cl_ref[...], zc_ref[...]) - _dot(sl_ref[...], zs_ref[...])) * scale
    y = _dot(mixed.astype(BF16), fw_ref[...]) + fb_ref[...]
    o_ref[...] = (y * _silu(g_ref[...].astype(F32))).astype(o_ref.dtype)


def _fnet(p2, p3, fw, fb, *, u_col0, g_col0, tm):
    t = p2.shape[0]
    bsz, length, _ = p3.shape
    groups, gd, _ = fw.shape
    branch = groups * gd
    tm = min(tm, t)
    cc, sc = _dft_cos_sin(gd)
    cl, sl = _dft_cos_sin(length)
    const2 = lambda shape: pl.BlockSpec(shape, lambda i: (0, 0))
    zc, zs = pl.pallas_call(
        functools.partial(_fnet_cols_kernel, group=gd),
        grid=(t // tm,),
        in_specs=[pl.BlockSpec((tm, branch), lambda i: (i, u_col0 // branch)),
                  const2((gd, gd)), const2((gd, gd))],
        out_specs=[pl.BlockSpec((tm, branch), lambda i: (i, 0))] * 2,
        out_shape=[jax.ShapeDtypeStruct((t, branch), BF16)] * 2,
        compiler_params=_params("parallel"), name="fnet_cols",
    )(p2, cc, sc)
    zc = zc.reshape(bsz, length, branch)
    zs = zs.reshape(bsz, length, branch)
    gc0 = g_col0 // gd
    seq = lambda c0: pl.BlockSpec((None, length, gd), lambda b, g: (b, 0, c0 + g))
    const = pl.BlockSpec((length, length), lambda b, g: (0, 0))
    return pl.pallas_call(
        functools.partial(_fnet_rows_kernel, scale=float((length * gd) ** -0.5)),
        grid=(bsz, groups),
        in_specs=[const, const, seq(0), seq(0),
                  pl.BlockSpec((None, gd, gd), lambda b, g: (g, 0, 0)),
                  pl.BlockSpec((1, gd), lambda b, g: (0, g)),
                  seq(gc0)],
        out_specs=seq(0),
        out_shape=jax.ShapeDtypeStruct((bsz, length, branch), BF16),
        compiler_params=_params("parallel", "parallel"), name="fnet_rows",
    )(cl, sl, zc, zs, fw, fb, p3)


def _conv3_rows(load_rows, r0, n_rows, length, w):
    h = BF16_SUBLANES
    u = load_rows(r0, n_rows)
    prev_blk = load_rows(pl.multiple_of(jnp.maximum(r0 - h, 0), h), h)
    next_blk = load_rows(pl.multiple_of(jnp.minimum(r0 + n_rows, length - h), h), h)
    prev_row = jnp.where(r0 > 0, prev_blk[h - 1:h, :], 0.0)
    next_row = jnp.where(r0 + n_rows < length, next_blk[0:1, :], 0.0)
    ri = lax.broadcasted_iota(jnp.int32, u.shape, 0)
    up = jnp.where(ri == 0, prev_row, pltpu.roll(u, 1, 0))
    un = jnp.where(ri == n_rows - 1, next_row, pltpu.roll(u, n_rows - 1, 0))
    return up * w[0:1, :] + u * w[1:2, :] + un * w[2:3, :]


def _sconv_kernel(cin_ref, cb_ref, cc_ref, cg_ref, w_ref, o_ref, *, rows_per_step):
    length, width = o_ref.shape

    def body(r, carry):
        r0 = pl.multiple_of(r * rows_per_step, rows_per_step)
        rows = pl.ds(r0, rows_per_step)
        for j in range(width // LANES):
            cols = slice(j * LANES, (j + 1) * LANES)

            def load_u(start, size, cols=cols):
                rs = pl.ds(start, size)
                return cc_ref[rs, cols].astype(F32) * cin_ref[rs, cols].astype(F32)

            conv = _conv3_rows(load_u, r0, rows_per_step, length, w_ref[:, cols])
            o_ref[rows, cols] = (cb_ref[rows, cols].astype(F32) * conv
                                 * _silu(cg_ref[rows, cols].astype(F32))).astype(o_ref.dtype)
        return carry

    lax.fori_loop(0, length // rows_per_step, body, 0)


def _sconv(p3, w, *, tn):
    bsz, length, _ = p3.shape
    branch = w.shape[1]
    nb = branch // tn
    seq = lambda k: pl.BlockSpec((None, length, tn), lambda b, j, k=k: (b, 0, k * nb + j))
    return pl.pallas_call(
        functools.partial(_sconv_kernel, rows_per_step=min(128, length)),
        grid=(bsz, nb),
        in_specs=[seq(0), seq(1), seq(2), seq(3), pl.BlockSpec((3, tn), lambda b, j: (0, j))],
        out_specs=seq(0),
        out_shape=jax.ShapeDtypeStruct((bsz, length, branch), BF16),
        compiler_params=_params("parallel", "parallel"), name="sconv",
    )(p3, p3, p3, p3, w)


def _ssd_kernel(x_ref, bm_ref, cm_ref, z_ref, dt_ref,
                wx_ref, wb_ref, wc_ref, bx_ref, bb_ref, bc_ref,
                dtb_ref, a_ref, dsk_ref,
                o_ref,
                xs_s, bmt_s, cm_s, dt_s, y_s, st_s, *, chunk):
    length, gw = xs_s.shape
    n_chunks = length // chunk
    heads = SSD_GROUP_HEADS
    hd = gw // heads

    def conv_silu(ref, w_ref, b_ref, r0, cols):
        load = lambda start, size: ref[pl.ds(start, size), cols].astype(F32)
        return _silu(_conv3_rows(load, r0, chunk, length, w_ref[:, cols]) + b_ref[:, cols])

    def pre(c, carry):
        r0 = pl.multiple_of(c * chunk, chunk)
        rows = pl.ds(r0, chunk)
        for j in range(gw // LANES):
            cols = slice(j * LANES, (j + 1) * LANES)
            xs_s[rows, cols] = conv_silu(x_ref, wx_ref, bx_ref, r0, cols)
        one = slice(0, LANES)
        cm_s[rows, :] = conv_silu(cm_ref, wc_ref, bc_ref, r0, one).astype(BF16)
        bmt_s[c] = conv_silu(bm_ref, wb_ref, bb_ref, r0, one).T.astype(BF16)
        t = dt_ref[rows, :] + dtb_ref[...]
        dt_s[rows, :] = jnp.maximum(t, 0.0) + jnp.log1p(jnp.exp(-jnp.abs(t)))
        return carry

    lax.fori_loop(0, n_chunks, pre, 0)

    ri = lax.broadcasted_iota(jnp.int32, (chunk, chunk), 0)
    ci = lax.broadcasted_iota(jnp.int32, (chunk, chunk), 1)
    lane = lax.broadcasted_iota(jnp.int32, (chunk, LANES), 1)
    sel_l = lax.broadcasted_iota(jnp.int32, (LANES, gw), 0)
    sel_c = lax.broadcasted_iota(jnp.int32, (LANES, gw), 1) // hd
    dsk = dsk_ref[...]

    for backward in (False, True):
        d = 1 if backward else 0
        mask = (ri <= ci) if backward else (ri >= ci)
        tri = jnp.where(mask, 1.0, 0.0).astype(BF16)
        tri = jnp.concatenate([tri, tri], axis=1)
        last = 0 if backward else chunk - 1
        expand = jnp.where(sel_l == sel_c + d * heads, 1.0, 0.0).astype(BF16)
        expand = jnp.concatenate([expand, expand], axis=0)
        a_row = a_ref[...]
        st_s[...] = jnp.zeros_like(st_s)

        def body(c, carry, d=d, backward=backward, mask=mask, tri=tri, last=last,
                 expand=expand, a_row=a_row):
            cc = (n_chunks - 1 - c) if backward else c
            rows = pl.ds(pl.multiple_of(cc * chunk, chunk), chunk)
            dtc = dt_s[rows, :]
            acum_c = _sel_dot_rhs(tri, dtc * a_row)
            acum = _sel_dot_lhs(acum_c, expand)
            dte = _sel_dot_lhs(dtc, expand)
            acum_t = acum_c.T
            xs_c = xs_s[rows, :]
            xdt = xs_c * dte
            cm_c = cm_s[rows, :]
            bmt_c = bmt_s[cc]
            cb = _dot(cm_c, bmt_c)
            state = st_s[...]
            y = _dot(cm_c, state.astype(BF16)) * jnp.exp(acum)
            a_last = acum[last:last + 1, :]
            to_end = jnp.exp(a_last - acum)
            st_s[...] = state * jnp.exp(a_last) + _dot(bmt_c, (xdt * to_end).astype(BF16))
            pairs = []
            for p in range(heads // 2):
                ms = []
                for hh in (2 * p, 2 * p + 1):
                    ln = d * heads + hh
                    diff = acum_c[:, ln:ln + 1] - acum_t[ln:ln + 1, :]
                    decay = jnp.exp(jnp.where(mask, diff, -1e30))
                    ms.append((cb * decay).astype(BF16))
                xp = xdt[:, p * LANES:(p + 1) * LANES]
                top = jnp.where(lane < hd, xp, 0.0).astype(BF16)
                bot = jnp.where(lane >= hd, xp, 0.0).astype(BF16)
                pairs.append(_dot(jnp.concatenate(ms, axis=1), jnp.concatenate([top, bot], axis=0)))
            y = y + jnp.concatenate(pairs, axis=1)
            if backward:
                tot = y_s[rows, :] + y + xs_c * dsk
                o_ref[rows, :] = (tot * _silu(z_ref[rows, :].astype(F32))).astype(o_ref.dtype)
            else:
                y_s[rows, :] = y
            return carry

        lax.fori_loop(0, n_chunks, body, 0)


def _ssd(p3, dt3, conv_w, conv_b, dt_bias_g, a_g, dsk, *, x_col0, z_col0, chunk):
    bsz, length, _ = p3.shape
    groups = dt3.shape[2] // LANES
    gw, ns = SSD_GROUP_W, SSD_STATE
    branch = groups * gw
    xb, bb, cb = x_col0 // gw, (x_col0 + branch) // ns, (x_col0 + branch + groups * ns) // ns
    seq = lambda w, c0: pl.BlockSpec((None, length, w), lambda b, g: (b, 0, c0 + g))
    par = lambda rows, w, c0: pl.BlockSpec((rows, w), lambda b, g: (0, c0 + g))
    n_chunks = length // chunk
    return pl.pallas_call(
        functools.partial(_ssd_kernel, chunk=chunk),
        grid=(bsz, groups),
        in_specs=[seq(gw, xb), seq(ns, bb), seq(ns, cb), seq(gw, z_col0 // gw), seq(LANES, 0),
                  par(3, gw, 0), par(3, ns, branch // ns), par(3, ns, (branch + groups * ns) // ns),
                  par(1, gw, 0), par(1, ns, branch // ns), par(1, ns, (branch + groups * ns) // ns),
                  par(1, LANES, 0), par(1, LANES, 0),
                  par(1, gw, 0)],
        out_specs=seq(gw, 0),
        out_shape=jax.ShapeDtypeStruct((bsz, length, branch), BF16),
        scratch_shapes=[pltpu.VMEM((length, gw), F32),
                        pltpu.VMEM((n_chunks, ns, chunk), BF16),
                        pltpu.VMEM((length, ns), BF16),
                        pltpu.VMEM((length, LANES), F32),
                        pltpu.VMEM((length, gw), F32),
                        pltpu.VMEM((ns, gw), F32)],
        compiler_params=_params("parallel", "parallel"), name="ssd",
    )(p3, p3, p3, p3, dt3, conv_w, conv_w, conv_w, conv_b, conv_b, conv_b,
      dt_bias_g, a_g, dsk)


def _ssd_group_lanes(v, groups):
    hpg = v.shape[1] // groups
    per = jnp.concatenate([v[0].reshape(groups, hpg), v[1].reshape(groups, hpg)], axis=1)
    per = jnp.pad(per, ((0, 0), (0, LANES - 2 * hpg)))
    return per.reshape(1, groups * LANES)


def kernel(x, norm_w, final_norm_w, ev_w_in, ev_w_out, hgrn_lb_logits, hgrn_norm_w, fnet_w, fnet_b,
           od_w_in, od_w_out, sconv_w, ssd_conv_w, ssd_conv_b, ssd_dt_bias, ssd_a_log, ssd_d, ssd_norm_w):
    bsz, length, d = x.shape
    t = bsz * length
    branch = d
    x2 = x.reshape(t, d)
    row = lambda v: v.astype(F32).reshape(1, -1)

    lower_bounds = jnp.cumsum(jax.nn.softmax(hgrn_lb_logits.astype(F32), axis=0), axis=0)
    p = _inproj(x2, row(norm_w[0]), ev_w_in[0].astype(BF16), tm=1024, tn=1024)
    p3 = p.reshape(bsz, length, -1)
    a_out = _hgrn(p3, row(lower_bounds[0]), row(hgrn_norm_w[0]), col0=0, chunk=HGRN_CHUNK)
    b_out = _fnet(p, p3, fnet_w[0].astype(BF16), row(fnet_b[0]),
                  u_col0=5 * branch, g_col0=6 * branch, tm=512)
    x2 = _outproj(a_out.reshape(t, branch), b_out.reshape(t, branch), x2,
                  ev_w_out[0].astype(BF16), tm=512)

    heads = ssd_d.shape[1]
    groups = heads // SSD_GROUP_HEADS
    n_main = 5 * branch + branch + 2 * groups * SSD_STATE
    w_in = od_w_in[0]
    w_dt = w_in[:, n_main:].reshape(d, 2, groups, SSD_GROUP_HEADS)
    w_dt = jnp.transpose(w_dt, (0, 2, 1, 3)).reshape(d, groups, 2 * SSD_GROUP_HEADS)
    w_dt = jnp.pad(w_dt, ((0, 0), (0, 0), (0, LANES - 2 * SSD_GROUP_HEADS))).reshape(d, groups * LANES)
    p, dt = _inproj(x2, row(norm_w[1]), w_in[:, :n_main].astype(BF16), w_dt.astype(BF16),
                    tm=1024, tn=1024)
    p3 = p.reshape(bsz, length, n_main)
    c_out = _sconv(p3, sconv_w[0].astype(F32), tn=512)
    a_neg = -jnp.exp(ssd_a_log[0].astype(F32))
    d_out = _ssd(p3, dt.reshape(bsz, length, groups * LANES),
                 ssd_conv_w[0].astype(F32), row(ssd_conv_b[0]),
                 _ssd_group_lanes(ssd_dt_bias[0].astype(F32), groups),
                 _ssd_group_lanes(a_neg, groups),
                 row(jnp.repeat(ssd_d[0].astype(F32), SSD_HEAD)),
                 x_col0=5 * branch, z_col0=4 * branch, chunk=SSD_CHUNK)
    out = _outproj(c_out.reshape(t, branch), d_out.reshape(t, branch), x2,
                   od_w_out[0].astype(BF16), row(ssd_norm_w[0]), row(final_norm_w), tm=512)
    return out.reshape(bsz, length, d)
```

```python
import functools

import jax
import jax.numpy as jnp
from jax import lax
from jax.experimental import pallas as pl
from jax.experimental.pallas import tpu as pltpu

F32 = jnp.float32
BF16 = jnp.bfloat16
EPS = 1e-6
LOG2E = 1.4426950408889634

VMEM_LIMIT_BYTES = 56 * 1024 * 1024
LANES = 128
BF16_SUBLANES = 16

HGRN_HEAD = 128
HGRN_CHUNK = 64
FNET_GROUP = 256
SSD_HEAD = 64
SSD_STATE = 128
SSD_GROUP_HEADS = 8
SSD_GROUP_W = SSD_HEAD * SSD_GROUP_HEADS
SSD_CHUNK = 128


def _params(*sem):
    return pltpu.CompilerParams(dimension_semantics=sem, vmem_limit_bytes=VMEM_LIMIT_BYTES)


def _dot(a, b):
    return jnp.dot(a, b, preferred_element_type=F32)


def _dot_nt(a, b):
    return lax.dot_general(a, b, (((1,), (1,)), ((), ())), preferred_element_type=F32)


def _dot_tn(a, b):
    return lax.dot_general(a, b, (((0,), (0,)), ((), ())), preferred_element_type=F32)


def _split2(x):
    hi = x.astype(BF16)
    lo = (x - hi.astype(F32)).astype(BF16)
    return hi, lo


def _sel_dot_rhs(sel_sel, x):
    return _dot(sel_sel, jnp.concatenate(_split2(x), axis=0))


def _sel_dot_lhs(x, sel_sel):
    return _dot(jnp.concatenate(_split2(x), axis=1), sel_sel)


def _silu(x):
    return x * jax.nn.sigmoid(x)


def _rms_scale(x):
    return lax.rsqrt(jnp.mean(x * x, axis=-1, keepdims=True) + EPS)


def _norm_rows_to(h_ref, x_ref, nw_ref, rows_per_step):
    nw = nw_ref[...]

    def body(r, carry):
        rows = pl.ds(pl.multiple_of(r * rows_per_step, rows_per_step), rows_per_step)
        x = x_ref[rows, :]
        h_ref[rows, :] = (x * _rms_scale(x) * nw).astype(BF16)
        return carry

    lax.fori_loop(0, x_ref.shape[0] // rows_per_step, body, 0)


def _inproj_kernel(x_ref, nw_ref, w_ref, o_ref, h_ref):
    @pl.when(pl.program_id(1) == 0)
    def _():
        _norm_rows_to(h_ref, x_ref, nw_ref, 128)

    o_ref[...] = _dot(h_ref[...], w_ref[...]).astype(o_ref.dtype)


def _inproj_dt_kernel(x_ref, nw_ref, w_ref, wdt_ref, o_ref, dt_ref, h_ref):
    @pl.when(pl.program_id(1) == 0)
    def _():
        _norm_rows_to(h_ref, x_ref, nw_ref, 128)
        dt_ref[...] = _dot(h_ref[...], wdt_ref[...])

    o_ref[...] = _dot(h_ref[...], w_ref[...]).astype(o_ref.dtype)


def _inproj(x2, nw, w, wdt=None, *, n, tm, tn):
    t, d = x2.shape
    tm = min(tm, t)
    grid = (t // tm, n // tn)
    in_specs = [
        pl.BlockSpec((tm, d), lambda i, j: (i, 0)),
        pl.BlockSpec((1, d), lambda i, j: (0, 0)),
        pl.BlockSpec((d, tn), lambda i, j: (0, j)),
    ]
    out_main = pl.BlockSpec((tm, tn), lambda i, j: (i, j))
    scratch = [pltpu.VMEM((tm, d), BF16)]
    if wdt is None:
        return pl.pallas_call(
            _inproj_kernel, grid=grid, in_specs=in_specs, out_specs=out_main,
            out_shape=jax.ShapeDtypeStruct((t, n), BF16), scratch_shapes=scratch,
            compiler_params=_params("parallel", "arbitrary"), name="inproj",
        )(x2, nw, w)
    ndt = wdt.shape[1]
    return pl.pallas_call(
        _inproj_dt_kernel, grid=grid,
        in_specs=in_specs + [pl.BlockSpec((d, ndt), lambda i, j: (0, 0))],
        out_specs=[out_main, pl.BlockSpec((tm, ndt), lambda i, j: (i, 0))],
        out_shape=[jax.ShapeDtypeStruct((t, n), BF16), jax.ShapeDtypeStruct((t, ndt), F32)],
        scratch_shapes=scratch,
        compiler_params=_params("parallel", "arbitrary"), name="inproj_dt",
    )(x2, nw, w, wdt)


def _outproj_kernel(*refs, norm_b, final_norm):
    a_ref, b_ref, x_ref, w_ref = refs[:4]
    rest = list(refs[4:])
    o_ref = rest.pop()
    ka = a_ref.shape[1]
    b = b_ref[...]
    if norm_b:
        bnw_ref = rest.pop(0)
        bf = b.astype(F32)
        b = (bf * _rms_scale(bf) * bnw_ref[...]).astype(BF16)
    acc = _dot(a_ref[...], w_ref[0:ka, :]) + _dot(b, w_ref[ka:, :]) + x_ref[...]
    if final_norm:
        fnw_ref = rest.pop(0)
        acc = acc * _rms_scale(acc) * fnw_ref[...]
    o_ref[...] = acc


def _outproj(a, b, x2, w, b_norm_w=None, final_norm_w=None, *, tm):
    t, d = x2.shape
    ka, kb = a.shape[1], b.shape[1]
    tm = min(tm, t)
    row = lambda width: pl.BlockSpec((tm, width), lambda i: (i, 0))
    const = lambda shape: pl.BlockSpec(shape, lambda i: (0, 0))
    in_specs = [row(ka), row(kb), row(d), const((ka + kb, d))]
    args = [a, b, x2, w]
    if b_norm_w is not None:
        in_specs.append(const((1, kb)))
        args.append(b_norm_w)
    if final_norm_w is not None:
        in_specs.append(const((1, d)))
        args.append(final_norm_w)
    kern = functools.partial(_outproj_kernel, norm_b=b_norm_w is not None,
                             final_norm=final_norm_w is not None)
    return pl.pallas_call(
        kern, grid=(t // tm,), in_specs=in_specs, out_specs=row(d),
        out_shape=jax.ShapeDtypeStruct((t, d), F32),
        compiler_params=_params("parallel"), name="outproj",
    )(*args)


def _hgrn_kernel(q_ref, v_ref, zf_ref, zb_ref, g_ref, lb_ref, nw_ref, o_ref,
                 qin_s, kin_s, qdec_s, kend_s, v_s, vt_s, dec_s, acc_s, *, chunk, block, n_heads):
    length, width = acc_s.shape
    n_chunks = length // chunk
    n_blocks = length // block
    per_block = block // chunk
    mid = chunk // 2
    z_refs = (zf_ref, zb_ref)
    edge = (chunk - 1, 0)

    def prepare(blk, st):
        lb = lb_ref[...]
        one_m_lb = 1.0 - lb
        bri = lax.broadcasted_iota(jnp.int32, (block, block), 0)
        bci = lax.broadcasted_iota(jnp.int32, (block, block), 1)
        same_chunk = (bri // chunk) == (bci // chunk)
        r0 = pl.multiple_of(blk * block, block)
        rows = pl.ds(r0, block)
        q = q_ref[rows, :].astype(F32)
        v = v_ref[rows, :]
        v_s[st, rows, :] = v
        vf = v.astype(F32)
        for k in range(per_block):
            vt_s[st, blk * per_block + k] = vf[k * chunk:(k + 1) * chunk, :].T.astype(BF16)
        for d in (0, 1):
            tri = jnp.where(same_chunk & ((bri <= bci) if d else (bri >= bci)), 1.0, 0.0).astype(BF16)
            z = z_refs[d][rows, :].astype(F32)
            sg = jax.nn.sigmoid(z)
            logf = jnp.log(lb + one_m_lb * sg)
            key = one_m_lb * (1.0 - sg)
            b2 = _dot(tri, jnp.concatenate(_split2(logf), axis=1))
            b = b2[:, :width] + b2[:, width:]
            for k in range(per_block):
                sl = slice(k * chunk, (k + 1) * chunk)
                rk = pl.ds(r0 + k * chunk, chunk)
                bk = b[sl]
                b_mid = bk[mid:mid + 1, :]
                b_edge = bk[edge[d]:edge[d] + 1, :]
                q_in = q[sl] * jnp.exp(bk - b_mid)
                k_in = key[sl] * jnp.exp(b_mid - bk)
                qin_s[st, d, rk, :] = q_in.astype(BF16)
                kin_s[st, d, rk, :] = k_in.astype(BF16)
                qdec_s[st, d, rk, :] = (q_in * jnp.exp(b_mid)).astype(BF16)
                kend_s[st, d, rk, :] = (k_in * jnp.exp(b_edge - b_mid)).astype(BF16)
                dec_s[st, d * n_chunks + blk * per_block + k] = jnp.broadcast_to(jnp.exp(b_edge), (8, width))

    def scan(c, states, st, finalize):
        ri = lax.broadcasted_iota(jnp.int32, (chunk, chunk), 0)
        ci = lax.broadcasted_iota(jnp.int32, (chunk, chunk), 1)
        new_states = []
        for d in (0, 1):
            cc = c if d == 0 else n_chunks - 1 - c
            rows = pl.ds(pl.multiple_of(cc * chunk, chunk), chunk)
            s = _dot_nt(qin_s[st, d, rows, :], kin_s[st, d, rows, :])
            s = jnp.where((ri <= ci) if d else (ri >= ci), s, 0.0).astype(BF16)
            o = _dot(s, v_s[st, rows, :]) + _dot_nt(qdec_s[st, d, rows, :], states[d].astype(BF16))
            new_states.append(states[d] * dec_s[st, d * n_chunks + cc][0:1, :]
                              + _dot(vt_s[st, cc], kend_s[st, d, rows, :]))
            if finalize:
                tot = acc_s[rows, :] + o
                g = g_ref[rows, :].astype(F32)
                o_ref[rows, :] = (tot * _rms_scale(tot) * nw_ref[...] * _silu(g)).astype(o_ref.dtype)
            else:
                acc_s[rows, :] = o
        return tuple(new_states)

    half = n_chunks // 2
    unroll = min(8, half)
    trips = half // unroll
    blocks_per_trip = n_blocks // (2 * trips)

    def run(prep_set, scan_set):
        def trip(i, states, *, finalize):
            if scan_set is not None:
                for u in range(unroll):
                    states = scan(i * unroll + u, states, scan_set, finalize)
            if prep_set is not None:
                for u in range(blocks_per_trip):
                    prepare(i * blocks_per_trip + u, prep_set)
            return states

        zero = jnp.zeros((width, width), F32)
        states = lax.fori_loop(0, trips, functools.partial(trip, finalize=False), (zero, zero))
        lax.fori_loop(trips, 2 * trips, functools.partial(trip, finalize=True), states)

    step = pl.program_id(0)
    inner = (step > 0) & (step < n_heads)
    pl.when(step == 0)(lambda: run(0, None))
    pl.when(inner & (step % 2 == 1))(lambda: run(1, 0))
    pl.when(inner & (step % 2 == 0))(lambda: run(0, 1))
    pl.when(step == n_heads)(lambda: run(None, (n_heads - 1) % 2))


def _hgrn(p3, lb, nw, *, col0, chunk):
    bsz, length, _ = p3.shape
    width = HGRN_HEAD
    branch = lb.shape[1]
    heads = branch // width
    n_heads = bsz * heads
    c0 = col0 // width
    n_chunks = length // chunk
    cur = lambda s: jnp.minimum(s, n_heads - 1)
    prev = lambda s: jnp.maximum(s - 1, 0)
    seq = lambda k, at: pl.BlockSpec((None, length, width),
                                     lambda s: (at(s) // heads, 0, c0 + k * heads + at(s) % heads))
    vec = lambda at: pl.BlockSpec((1, width), lambda s: (0, at(s) % heads))
    scaled = pltpu.VMEM((2, 2, length, width), BF16)
    return pl.pallas_call(
        functools.partial(_hgrn_kernel, chunk=chunk, block=min(128, length), n_heads=n_heads),
        grid=(n_heads + 1,),
        in_specs=[seq(0, cur), seq(1, cur), seq(2, cur), seq(3, cur), seq(4, prev), vec(cur), vec(prev)],
        out_specs=pl.BlockSpec((None, length, width), lambda s: (prev(s) // heads, 0, prev(s) % heads)),
        out_shape=jax.ShapeDtypeStruct((bsz, length, branch), BF16),
        scratch_shapes=[scaled, scaled, scaled, scaled,
                        pltpu.VMEM((2, length, width), BF16),
                        pltpu.VMEM((2, n_chunks, width, chunk), BF16),
                        pltpu.VMEM((2, 2 * n_chunks, 8, width), F32),
                        pltpu.VMEM((length, width), F32)],
        compiler_params=_params("arbitrary"), name="hgrn2",
    )(p3, p3, p3, p3, p3, lb, nw)


def _dft_cos_sin(n):
    k = lax.broadcasted_iota(jnp.int32, (n, n), 0) * lax.broadcasted_iota(jnp.int32, (n, n), 1)
    ang = (k % n).astype(F32) * (2.0 * jnp.pi / n)
    return jnp.cos(ang).astype(BF16), jnp.sin(ang).astype(BF16)


def _fnet_cols_kernel(u_ref, cc_ref, sc_ref, zc_ref, zs_ref, *, group):
    for g in range(u_ref.shape[1] // group):
        cols = slice(g * group, (g + 1) * group)
        u = u_ref[:, cols]
        zc_ref[:, cols] = _dot(u, cc_ref[...]).astype(BF16)
        zs_ref[:, cols] = _dot(u, sc_ref[...]).astype(BF16)


def _fnet_rows_kernel(cl_ref, sl_ref, zc_ref, zs_ref, fw_ref, fb_ref, g_ref, o_ref, *, scale):
    mixed = (_dot(cl_ref[...], zc_ref[...]) - _dot(sl_ref[...], zs_ref[...])) * scale
    y = _dot(mixed.astype(BF16), fw_ref[...]) + fb_ref[...]
    o_ref[...] = (y * _silu(g_ref[...].astype(F32))).astype(o_ref.dtype)


def _fnet(p2, p3, fw, fb, *, u_col0, g_col0, tm):
    t = p2.shape[0]
    bsz, length, _ = p3.shape
    groups, gd, _ = fw.shape
    branch = groups * gd
    tm = min(tm, t)
    cc, sc = _dft_cos_sin(gd)
    cl, sl = _dft_cos_sin(length)
    const2 = lambda shape: pl.BlockSpec(shape, lambda i: (0, 0))
    zc, zs = pl.pallas_call(
        functools.partial(_fnet_cols_kernel, group=gd),
        grid=(t // tm,),
        in_specs=[pl.BlockSpec((tm, branch), lambda i: (i, u_col0 // branch)),
                  const2((gd, gd)), const2((gd, gd))],
        out_specs=[pl.BlockSpec((tm, branch), lambda i: (i, 0))] * 2,
        out_shape=[jax.ShapeDtypeStruct((t, branch), BF16)] * 2,
        compiler_params=_params("parallel"), name="fnet_cols",
    )(p2, cc, sc)
    zc = zc.reshape(bsz, length, branch)
    zs = zs.reshape(bsz, length, branch)
    gc0 = g_col0 // gd
    seq = lambda c0: pl.BlockSpec((None, length, gd), lambda b, g: (b, 0, c0 + g))
    const = pl.BlockSpec((length, length), lambda b, g: (0, 0))
    return pl.pallas_call(
        functools.partial(_fnet_rows_kernel, scale=float((length * gd) ** -0.5)),
        grid=(bsz, groups),
        in_specs=[const, const, seq(0), seq(0),
                  pl.BlockSpec((None, gd, gd), lambda b, g: (g, 0, 0)),
                  pl.BlockSpec((1, gd), lambda b, g: (0, g)),
                  seq(gc0)],
        out_specs=seq(0),
        out_shape=jax.ShapeDtypeStruct((bsz, length, branch), BF16),
        compiler_params=_params("parallel", "parallel"), name="fnet_rows",
    )(cl, sl, zc, zs, fw, fb, p3)


def _conv3_rows(load_rows, r0, n_rows, length, w):
    h = BF16_SUBLANES
    u = load_rows(r0, n_rows)
    prev_blk = load_rows(pl.multiple_of(jnp.maximum(r0 - h, 0), h), h)
    next_blk = load_rows(pl.multiple_of(jnp.minimum(r0 + n_rows, length - h), h), h)
    prev_row = jnp.where(r0 > 0, prev_blk[h - 1:h, :], 0.0)
    next_row = jnp.where(r0 + n_rows < length, next_blk[0:1, :], 0.0)
    ri = lax.broadcasted_iota(jnp.int32, u.shape, 0)
    up = jnp.where(ri == 0, prev_row, pltpu.roll(u, 1, 0))
    un = jnp.where(ri == n_rows - 1, next_row, pltpu.roll(u, n_rows - 1, 0))
    return up * w[0:1, :] + u * w[1:2, :] + un * w[2:3, :]


def _sconv_kernel(cin_ref, cb_ref, cc_ref, cg_ref, w_ref, o_ref, *, rows_per_step):
    length, width = o_ref.shape

    def body(r, carry):
        r0 = pl.multiple_of(r * rows_per_step, rows_per_step)
        rows = pl.ds(r0, rows_per_step)
        for j in range(width // LANES):
            cols = slice(j * LANES, (j + 1) * LANES)

            def load_u(start, size, cols=cols):
                rs = pl.ds(start, size)
                return cc_ref[rs, cols].astype(F32) * cin_ref[rs, cols].astype(F32)

            conv = _conv3_rows(load_u, r0, rows_per_step, length, w_ref[:, cols])
            o_ref[rows, cols] = (cb_ref[rows, cols].astype(F32) * conv
                                 * _silu(cg_ref[rows, cols].astype(F32))).astype(o_ref.dtype)
        return carry

    lax.fori_loop(0, length // rows_per_step, body, 0)


def _sconv(p3, w, *, tn):
    bsz, length, _ = p3.shape
    branch = w.shape[1]
    nb = branch // tn
    seq = lambda k: pl.BlockSpec((None, length, tn), lambda b, j, k=k: (b, 0, k * nb + j))
    return pl.pallas_call(
        functools.partial(_sconv_kernel, rows_per_step=min(128, length)),
        grid=(bsz, nb),
        in_specs=[seq(0), seq(1), seq(2), seq(3), pl.BlockSpec((3, tn), lambda b, j: (0, j))],
        out_specs=seq(0),
        out_shape=jax.ShapeDtypeStruct((bsz, length, branch), BF16),
        compiler_params=_params("parallel", "parallel"), name="sconv",
    )(p3, p3, p3, p3, w)


def _ssd_kernel(x_ref, bm_ref, cm_ref, z_ref, dt_ref,
                wx_ref, wb_ref, wc_ref, bx_ref, bb_ref, bc_ref,
                dtb_ref, a_ref, dsk_ref,
                o_ref,
                bmt_s, cm_s, yacc_s, ecol_s, xdte_s, alast_s, st_s, *, chunk):
    length, gw = yacc_s.shape
    n_chunks = length // chunk
    heads = SSD_GROUP_HEADS
    hd = gw // heads
    halo = BF16_SUBLANES
    ext = chunk + 2 * halo
    edge = (chunk - 1, 0)

    ri = lax.broadcasted_iota(jnp.int32, (chunk, chunk), 0)
    ci = lax.broadcasted_iota(jnp.int32, (chunk, chunk), 1)
    masks = (ri >= ci, ri <= ci)
    tris = []
    for m in masks:
        tri = jnp.where(m, 1.0, 0.0).astype(BF16)
        tris.append(jnp.concatenate([tri, tri], axis=1))
    low = lax.broadcasted_iota(jnp.int32, (chunk, LANES), 1).astype(F32).astype(BF16) < hd
    si = lax.broadcasted_iota(jnp.int32, (chunk, ext), 0)
    sj = lax.broadcasted_iota(jnp.int32, (chunk, ext), 1)
    sel_l = lax.broadcasted_iota(jnp.int32, (LANES, gw), 0)
    sel_c = lax.broadcasted_iota(jnp.int32, (LANES, gw), 1) // hd
    expands = []
    for d in (0, 1):
        e = jnp.where(sel_l == sel_c + d * heads, 1.0, 0.0).astype(BF16)
        expands.append(jnp.concatenate([e, e], axis=0))
    a2_row = a_ref[...] * LOG2E

    def prepare(c, carry):
        r0 = pl.multiple_of(c * chunk, chunk)
        rows = pl.ds(r0, chunk)
        start = pl.multiple_of(jnp.clip(r0 - halo, 0, length - ext), halo)
        off = r0 - start
        erows = pl.ds(start, ext)
        shift = jnp.concatenate([jnp.where(sj == si + (off - 1), 1.0, 0.0),
                                 jnp.where(sj == si + (off + 1), 1.0, 0.0)], axis=0).astype(BF16)

        def conv_silu(ref, w_ref, b_ref, cols):
            both = _dot(shift, ref[erows, cols])
            w = w_ref[:, cols]
            y = (both[:chunk] * w[0:1, :] + ref[rows, cols].astype(F32) * w[1:2, :]
                 + both[chunk:] * w[2:3, :] + b_ref[:, cols])
            return _silu(y)

        one = slice(0, LANES)
        cm_c = conv_silu(cm_ref, wc_ref, bc_ref, one).astype(BF16)
        bmt_c = conv_silu(bm_ref, wb_ref, bb_ref, one).T.astype(BF16)
        cm_s[rows, :] = cm_c
        bmt_s[c] = bmt_c
        cb = _dot(cm_c, bmt_c)
        t = dt_ref[rows, :] + dtb_ref[...]
        dtc = jnp.maximum(t, 0.0) + jnp.log1p(jnp.exp(-jnp.abs(t)))
        acum_c = [_sel_dot_rhs(tris[d], dtc * a2_row) for d in (0, 1)]
        acum_t = [a.T for a in acum_c]
        acum_hl = [jnp.concatenate(_split2(a), axis=1) for a in acum_c]
        dt_hl = jnp.concatenate(_split2(dtc), axis=1)
        wide = 2 * LANES
        for s in range(gw // wide):
            wcols = slice(s * wide, (s + 1) * wide)
            xs_w = conv_silu(x_ref, wx_ref, bx_ref, wcols)
            acum_w = [_dot(acum_hl[d], expands[d][:, wcols]) for d in (0, 1)]
            dt_w = [_dot(dt_hl, expands[d][:, wcols]) for d in (0, 1)]
            for p in (2 * s, 2 * s + 1):
                slab = slice(p * LANES, (p + 1) * LANES)
                sub = slice((p % 2) * LANES, (p % 2 + 1) * LANES)
                xs_p = xs_w[:, sub]
                acc = xs_p * dsk_ref[:, slab]
                for d in (0, 1):
                    l0 = d * heads + 2 * p
                    acum_p = acum_w[d][:, sub]
                    xdt = xs_p * dt_w[d][:, sub]
                    a_edge = acum_p[edge[d]:edge[d] + 1, :]
                    ecol_s[d, rows, slab] = jnp.exp2(acum_p).astype(BF16)
                    xdte_s[d, rows, slab] = (xdt * jnp.exp2(a_edge - acum_p)).astype(BF16)
                    alast_s[d * n_chunks + c, :, slab] = jnp.broadcast_to(jnp.exp2(a_edge), (8, LANES))
                    ms = []
                    for l in (l0, l0 + 1):
                        diff = acum_c[d][:, l:l + 1] - acum_t[d][l:l + 1, :]
                        ms.append((cb * jnp.exp2(jnp.where(masks[d], diff, -1e30))).astype(BF16))
                    xb = xdt.astype(BF16)
                    zero = jnp.zeros_like(xb)
                    block_diag = jnp.concatenate([jnp.where(low, xb, zero), jnp.where(low, zero, xb)], axis=0)
                    acc = acc + _dot(jnp.concatenate(ms, axis=1), block_diag)
                yacc_s[rows, slab] = acc
        return carry

    lax.fori_loop(0, n_chunks, prepare, 0)

    st_s[...] = jnp.zeros_like(st_s)

    def scan(c, carry, *, finalize):
        for d in (0, 1):
            cc = c if d == 0 else n_chunks - 1 - c
            rows = pl.ds(pl.multiple_of(cc * chunk, chunk), chunk)
            state = st_s[d]
            y = _dot(cm_s[rows, :], state.astype(BF16)) * ecol_s[d, rows, :].astype(F32)
            st_s[d] = state * alast_s[d * n_chunks + cc][0:1, :] + _dot(bmt_s[cc], xdte_s[d, rows, :])
            if finalize:
                tot = yacc_s[rows, :] + y
                o_ref[rows, :] = (tot * _silu(z_ref[rows, :].astype(F32))).astype(o_ref.dtype)
            else:
                yacc_s[rows, :] = yacc_s[rows, :] + y
        return carry

    half = n_chunks // 2
    lax.fori_loop(0, half, functools.partial(scan, finalize=False), 0)
    lax.fori_loop(half, n_chunks, functools.partial(scan, finalize=True), 0)


def _ssd(p3, dt3, conv_w, conv_b, dt_bias_g, a_g, dsk, *, x_col0, z_col0, chunk):
    bsz, length, _ = p3.shape
    groups = dt3.shape[2] // LANES
    gw, ns = SSD_GROUP_W, SSD_STATE
    branch = groups * gw
    xb, bb, cb = x_col0 // gw, (x_col0 + branch) // ns, (x_col0 + branch + groups * ns) // ns
    seq = lambda w, c0: pl.BlockSpec((None, length, w), lambda b, g: (b, 0, c0 + g))
    par = lambda rows, w, c0: pl.BlockSpec((rows, w), lambda b, g: (0, c0 + g))
    n_chunks = length // chunk
    return pl.pallas_call(
        functools.partial(_ssd_kernel, chunk=chunk),
        grid=(bsz, groups),
        in_specs=[seq(gw, xb), seq(ns, bb), seq(ns, cb), seq(gw, z_col0 // gw), seq(LANES, 0),
                  par(3, gw, 0), par(3, ns, branch // ns), par(3, ns, (branch + groups * ns) // ns),
                  par(1, gw, 0), par(1, ns, branch // ns), par(1, ns, (branch + groups * ns) // ns),
                  par(1, LANES, 0), par(1, LANES, 0),
                  par(1, gw, 0)],
        out_specs=seq(gw, 0),
        out_shape=jax.ShapeDtypeStruct((bsz, length, branch), BF16),
        scratch_shapes=[pltpu.VMEM((n_chunks, ns, chunk), BF16),
                        pltpu.VMEM((length, ns), BF16),
                        pltpu.VMEM((length, gw), F32),
                        pltpu.VMEM((2, length, gw), BF16),
                        pltpu.VMEM((2, length, gw), BF16),
                        pltpu.VMEM((2 * n_chunks, 8, gw), F32),
                        pltpu.VMEM((2, ns, gw), F32)],
        compiler_params=_params("parallel", "parallel"), name="ssd",
    )(p3, p3, p3, p3, dt3, conv_w, conv_w, conv_w, conv_b, conv_b, conv_b,
      dt_bias_g, a_g, dsk)


def _ssd_group_lanes(v, groups):
    hpg = v.shape[1] // groups
    per = jnp.concatenate([v[0].reshape(groups, hpg), v[1].reshape(groups, hpg)], axis=1)
    per = jnp.pad(per, ((0, 0), (0, LANES - 2 * hpg)))
    return per.reshape(1, groups * LANES)


def kernel(x, norm_w, final_norm_w, ev_w_in, ev_w_out, hgrn_lb_logits, hgrn_norm_w, fnet_w, fnet_b,
           od_w_in, od_w_out, sconv_w, ssd_conv_w, ssd_conv_b, ssd_dt_bias, ssd_a_log, ssd_d, ssd_norm_w):
    bsz, length, d = x.shape
    t = bsz * length
    branch = d
    x2 = x.reshape(t, d)
    row = lambda v: v.astype(F32).reshape(1, -1)

    lower_bounds = jnp.cumsum(jax.nn.softmax(hgrn_lb_logits.astype(F32), axis=0), axis=0)
    p = _inproj(x2, row(norm_w[0]), ev_w_in[0].astype(BF16), n=ev_w_in.shape[2], tm=1024, tn=1024)
    p3 = p.reshape(bsz, length, -1)
    a_out = _hgrn(p3, row(lower_bounds[0]), row(hgrn_norm_w[0]), col0=0, chunk=HGRN_CHUNK)
    b_out = _fnet(p, p3, fnet_w[0].astype(BF16), row(fnet_b[0]),
                  u_col0=5 * branch, g_col0=6 * branch, tm=512)
    x2 = _outproj(a_out.reshape(t, branch), b_out.reshape(t, branch), x2,
                  ev_w_out[0].astype(BF16), tm=512)

    heads = ssd_d.shape[1]
    groups = heads // SSD_GROUP_HEADS
    n_main = 5 * branch + branch + 2 * groups * SSD_STATE
    w_in = od_w_in[0]
    w_dt = w_in[:, n_main:].reshape(d, 2, groups, SSD_GROUP_HEADS)
    w_dt = jnp.transpose(w_dt, (0, 2, 1, 3)).reshape(d, groups, 2 * SSD_GROUP_HEADS)
    w_dt = jnp.pad(w_dt, ((0, 0), (0, 0), (0, LANES - 2 * SSD_GROUP_HEADS))).reshape(d, groups * LANES)
    p, dt = _inproj(x2, row(norm_w[1]), w_in.astype(BF16), w_dt.astype(BF16), n=n_main, tm=1024, tn=1024)
    p3 = p.reshape(bsz, length, n_main)
    c_out = _sconv(p3, sconv_w[0].astype(F32), tn=512)
    a_neg = -jnp.exp(ssd_a_log[0].astype(F32))
    d_out = _ssd(p3, dt.reshape(bsz, length, groups * LANES),
                 ssd_conv_w[0].astype(F32), row(ssd_conv_b[0]),
                 _ssd_group_lanes(ssd_dt_bias[0].astype(F32), groups),
                 _ssd_group_lanes(a_neg, groups),
                 row(jnp.repeat(ssd_d[0].astype(F32), SSD_HEAD)),
                 x_col0=5 * branch, z_col0=4 * branch, chunk=SSD_CHUNK)
    out = _outproj(c_out.reshape(t, branch), d_out.reshape(t, branch), x2,
                   od_w_out[0].astype(BF16), row(ssd_norm_w[0]), row(final_norm_w), tm=512)
    return out.reshape(bsz, length, d)
```

```python
import functools

import jax
import jax.numpy as jnp
from jax import lax
from jax.experimental import pallas as pl
from jax.experimental.pallas import tpu as pltpu

F32 = jnp.float32
BF16 = jnp.bfloat16
EPS = 1e-6
LOG2E = 1.4426950408889634

VMEM_LIMIT_BYTES = 56 * 1024 * 1024
LANES = 128
BF16_SUBLANES = 16

HGRN_HEAD = 128
HGRN_CHUNK = 64
FNET_GROUP = 256
FNET_TABLE_ROWS = 64
SSD_HEAD = 64
SSD_STATE = 128
SSD_GROUP_HEADS = 8
SSD_GROUP_W = SSD_HEAD * SSD_GROUP_HEADS
SSD_CHUNK = 128


def _params(*sem):
    return pltpu.CompilerParams(dimension_semantics=sem, vmem_limit_bytes=VMEM_LIMIT_BYTES)


def _dot(a, b):
    return jnp.dot(a, b, preferred_element_type=F32)


def _dot_nt(a, b):
    return lax.dot_general(a, b, (((1,), (1,)), ((), ())), preferred_element_type=F32)


def _dot_tn(a, b):
    return lax.dot_general(a, b, (((0,), (0,)), ((), ())), preferred_element_type=F32)


def _split2(x):
    hi = x.astype(BF16)
    lo = (x - hi.astype(F32)).astype(BF16)
    return hi, lo


def _sel_dot_rhs(sel_sel, x):
    return _dot(sel_sel, jnp.concatenate(_split2(x), axis=0))


def _sel_dot_lhs(x, sel_sel):
    return _dot(jnp.concatenate(_split2(x), axis=1), sel_sel)


def _silu(x):
    return x * jax.nn.sigmoid(x)


def _rms_scale(x):
    return lax.rsqrt(jnp.mean(x * x, axis=-1, keepdims=True) + EPS)


def _norm_rows_to(h_ref, x_ref, nw_ref, rows_per_step):
    nw = nw_ref[...]

    def body(r, carry):
        rows = pl.ds(pl.multiple_of(r * rows_per_step, rows_per_step), rows_per_step)
        x = x_ref[rows, :]
        h_ref[rows, :] = (x * _rms_scale(x) * nw).astype(BF16)
        return carry

    lax.fori_loop(0, x_ref.shape[0] // rows_per_step, body, 0)


def _inproj_kernel(x_ref, nw_ref, w_ref, o_ref, h_ref):
    @pl.when(pl.program_id(1) == 0)
    def _():
        _norm_rows_to(h_ref, x_ref, nw_ref, 128)

    o_ref[...] = _dot(h_ref[...], w_ref[...]).astype(o_ref.dtype)


def _inproj_dt_kernel(x_ref, nw_ref, w_ref, wdt_ref, o_ref, dt_ref, h_ref):
    @pl.when(pl.program_id(1) == 0)
    def _():
        _norm_rows_to(h_ref, x_ref, nw_ref, 128)
        dt_ref[...] = _dot(h_ref[...], wdt_ref[...])

    o_ref[...] = _dot(h_ref[...], w_ref[...]).astype(o_ref.dtype)


def _inproj(x2, nw, w, wdt=None, *, n, tm, tn):
    t, d = x2.shape
    tm = min(tm, t)
    grid = (t // tm, n // tn)
    in_specs = [
        pl.BlockSpec((tm, d), lambda i, j: (i, 0)),
        pl.BlockSpec((1, d), lambda i, j: (0, 0)),
        pl.BlockSpec((d, tn), lambda i, j: (0, j)),
    ]
    out_main = pl.BlockSpec((tm, tn), lambda i, j: (i, j))
    scratch = [pltpu.VMEM((tm, d), BF16)]
    if wdt is None:
        return pl.pallas_call(
            _inproj_kernel, grid=grid, in_specs=in_specs, out_specs=out_main,
            out_shape=jax.ShapeDtypeStruct((t, n), BF16), scratch_shapes=scratch,
            compiler_params=_params("parallel", "arbitrary"), name="inproj",
        )(x2, nw, w)
    ndt = wdt.shape[1]
    return pl.pallas_call(
        _inproj_dt_kernel, grid=grid,
        in_specs=in_specs + [pl.BlockSpec((d, ndt), lambda i, j: (0, 0))],
        out_specs=[out_main, pl.BlockSpec((tm, ndt), lambda i, j: (i, 0))],
        out_shape=[jax.ShapeDtypeStruct((t, n), BF16), jax.ShapeDtypeStruct((t, ndt), F32)],
        scratch_shapes=scratch,
        compiler_params=_params("parallel", "arbitrary"), name="inproj_dt",
    )(x2, nw, w, wdt)


def _outproj_kernel(*refs, norm_b, final_norm):
    a_ref, b_ref, x_ref, w_ref = refs[:4]
    rest = list(refs[4:])
    o_ref = rest.pop()
    ka = a_ref.shape[1]
    b = b_ref[...]
    if norm_b:
        bnw_ref = rest.pop(0)
        bf = b.astype(F32)
        b = (bf * _rms_scale(bf) * bnw_ref[...]).astype(BF16)
    acc = _dot(a_ref[...], w_ref[0:ka, :]) + _dot(b, w_ref[ka:, :]) + x_ref[...]
    if final_norm:
        fnw_ref = rest.pop(0)
        acc = acc * _rms_scale(acc) * fnw_ref[...]
    o_ref[...] = acc


def _outproj(a, b, x2, w, b_norm_w=None, final_norm_w=None, *, tm):
    t, d = x2.shape
    ka, kb = a.shape[1], b.shape[1]
    tm = min(tm, t)
    row = lambda width: pl.BlockSpec((tm, width), lambda i: (i, 0))
    const = lambda shape: pl.BlockSpec(shape, lambda i: (0, 0))
    in_specs = [row(ka), row(kb), row(d), const((ka + kb, d))]
    args = [a, b, x2, w]
    if b_norm_w is not None:
        in_specs.append(const((1, kb)))
        args.append(b_norm_w)
    if final_norm_w is not None:
        in_specs.append(const((1, d)))
        args.append(final_norm_w)
    kern = functools.partial(_outproj_kernel, norm_b=b_norm_w is not None,
                             final_norm=final_norm_w is not None)
    return pl.pallas_call(
        kern, grid=(t // tm,), in_specs=in_specs, out_specs=row(d),
        out_shape=jax.ShapeDtypeStruct((t, d), F32),
        compiler_params=_params("parallel"), name="outproj",
    )(*args)


def _hgrn_kernel(q_ref, v_ref, zf_ref, zb_ref, g_ref, lb_ref, nw_ref, o_ref,
                 qin_s, kin_s, qdec_s, kendt_s, v_s, dec_s, acc_s, *, chunk, block, n_heads):
    length, width = acc_s.shape
    n_chunks = length // chunk
    n_blocks = length // block
    per_block = block // chunk
    mid = chunk // 2
    z_refs = (zf_ref, zb_ref)
    edge = (chunk - 1, 0)

    def prepare(blk, st):
        lb = lb_ref[...]
        one_m_lb = 1.0 - lb
        bri = lax.broadcasted_iota(jnp.int32, (block, block), 0)
        bci = lax.broadcasted_iota(jnp.int32, (block, block), 1)
        same_chunk = (bri // chunk) == (bci // chunk)
        r0 = pl.multiple_of(blk * block, block)
        rows = pl.ds(r0, block)
        q = q_ref[rows, :].astype(F32)
        v_s[st, rows, :] = v_ref[rows, :]
        for d in (0, 1):
            tri = jnp.where(same_chunk & ((bri <= bci) if d else (bri >= bci)), 1.0, 0.0).astype(BF16)
            z = z_refs[d][rows, :].astype(F32)
            sg = jax.nn.sigmoid(z)
            logf = jnp.log2(lb + one_m_lb * sg)
            key = one_m_lb * (1.0 - sg)
            b2 = _dot(tri, jnp.concatenate(_split2(logf), axis=1))
            b = b2[:, :width] + b2[:, width:]
            for k in range(per_block):
                sl = slice(k * chunk, (k + 1) * chunk)
                rk = pl.ds(r0 + k * chunk, chunk)
                bk = b[sl]
                b_mid = bk[mid:mid + 1, :]
                b_edge = bk[edge[d]:edge[d] + 1, :]
                q_in = q[sl] * jnp.exp2(bk - b_mid)
                k_in = key[sl] * jnp.exp2(b_mid - bk)
                qin_s[st, d, rk, :] = q_in.astype(BF16)
                kin_s[st, d, rk, :] = k_in.astype(BF16)
                qdec_s[st, d, rk, :] = (q_in * jnp.exp2(b_mid)).astype(BF16)
                idx = d * n_chunks + blk * per_block + k
                kendt_s[st, idx] = (k_in * jnp.exp2(b_edge - b_mid)).T.astype(BF16)
                dec_s[st, idx] = jnp.broadcast_to(jnp.exp2(b_edge), (width, width)).T

    def scan(c, states, st, finalize):
        ri = lax.broadcasted_iota(jnp.int32, (chunk, chunk), 0)
        ci = lax.broadcasted_iota(jnp.int32, (chunk, chunk), 1)
        new_states = []
        for d in (0, 1):
            cc = c if d == 0 else n_chunks - 1 - c
            rows = pl.ds(pl.multiple_of(cc * chunk, chunk), chunk)
            idx = d * n_chunks + cc
            v = v_s[st, rows, :]
            s = _dot_nt(qin_s[st, d, rows, :], kin_s[st, d, rows, :])
            s = jnp.where((ri <= ci) if d else (ri >= ci), s, 0.0).astype(BF16)
            o = _dot(jnp.concatenate([qdec_s[st, d, rows, :], s], axis=1),
                     jnp.concatenate([states[d].astype(BF16), v], axis=0))
            new_states.append(states[d] * dec_s[st, idx] + _dot(kendt_s[st, idx], v))
            if finalize:
                tot = acc_s[rows, :] + o
                g = g_ref[rows, :].astype(F32)
                o_ref[rows, :] = (tot * _rms_scale(tot) * nw_ref[...] * _silu(g)).astype(o_ref.dtype)
            else:
                acc_s[rows, :] = o
        return tuple(new_states)

    half = n_chunks // 2
    unroll = min(8, half)
    trips = half // unroll
    blocks_per_trip = n_blocks // (2 * trips)

    def run(prep_set, scan_set):
        def trip(i, states, *, finalize):
            if scan_set is not None:
                for u in range(unroll):
                    states = scan(i * unroll + u, states, scan_set, finalize)
            if prep_set is not None:
                for u in range(blocks_per_trip):
                    prepare(i * blocks_per_trip + u, prep_set)
            return states

        zero = jnp.zeros((width, width), F32)
        states = lax.fori_loop(0, trips, functools.partial(trip, finalize=False), (zero, zero))
        lax.fori_loop(trips, 2 * trips, functools.partial(trip, finalize=True), states)

    step = pl.program_id(0)
    inner = (step > 0) & (step < n_heads)
    pl.when(step == 0)(lambda: run(0, None))
    pl.when(inner & (step % 2 == 1))(lambda: run(1, 0))
    pl.when(inner & (step % 2 == 0))(lambda: run(0, 1))
    pl.when(step == n_heads)(lambda: run(None, (n_heads - 1) % 2))


def _hgrn(p3, lb, nw, *, col0, chunk):
    bsz, length, _ = p3.shape
    width = HGRN_HEAD
    branch = lb.shape[1]
    heads = branch // width
    n_heads = bsz * heads
    c0 = col0 // width
    n_chunks = length // chunk
    cur = lambda s: jnp.minimum(s, n_heads - 1)
    prev = lambda s: jnp.maximum(s - 1, 0)
    seq = lambda k, at: pl.BlockSpec((None, length, width),
                                     lambda s: (at(s) // heads, 0, c0 + k * heads + at(s) % heads))
    vec = lambda at: pl.BlockSpec((1, width), lambda s: (0, at(s) % heads))
    scaled = pltpu.VMEM((2, 2, length, width), BF16)
    return pl.pallas_call(
        functools.partial(_hgrn_kernel, chunk=chunk, block=min(128, length), n_heads=n_heads),
        grid=(n_heads + 1,),
        in_specs=[seq(0, cur), seq(1, cur), seq(2, cur), seq(3, cur), seq(4, prev), vec(cur), vec(prev)],
        out_specs=pl.BlockSpec((None, length, width), lambda s: (prev(s) // heads, 0, prev(s) % heads)),
        out_shape=jax.ShapeDtypeStruct((bsz, length, branch), BF16),
        scratch_shapes=[scaled, scaled, scaled,
                        pltpu.VMEM((2, 2 * n_chunks, width, chunk), BF16),
                        pltpu.VMEM((2, length, width), BF16),
                        pltpu.VMEM((2, 2 * n_chunks, width, width), F32),
                        pltpu.VMEM((length, width), F32)],
        compiler_params=_params("arbitrary"), name="hgrn2",
    )(p3, p3, p3, p3, p3, lb, nw)


def _dft_cos_sin(n):
    k = lax.broadcasted_iota(jnp.int32, (n, n), 0) * lax.broadcasted_iota(jnp.int32, (n, n), 1)
    ang = (k % n).astype(F32) * (2.0 * jnp.pi / n)
    return jnp.cos(ang).astype(BF16), jnp.sin(ang).astype(BF16)


def _dft_rows(rows, n):
    k = rows[:, None] * lax.broadcasted_iota(jnp.int32, (rows.shape[0], n), 1)
    ang = (k % n).astype(F32) * (2.0 * jnp.pi / n)
    return jnp.cos(ang), jnp.sin(ang)


def _fnet_cols_kernel(u_ref, cc_ref, sc_ref, zc_ref, zs_ref, *, group):
    for g in range(u_ref.shape[1] // group):
        cols = slice(g * group, (g + 1) * group)
        u = u_ref[:, cols]
        zc_ref[:, cols] = _dot(u, cc_ref[...]).astype(BF16)
        zs_ref[:, cols] = _dot(u, sc_ref[...]).astype(BF16)


def _fnet_rows_kernel(t1c_ref, t1s_ref, t2c_ref, t2s_ref, zc_ref, zs_ref, fw_ref, fb_ref, g_ref, o_ref,
                      cl_s, sl_s, *, scale):
    rb = t1c_ref.shape[0]

    @pl.when((pl.program_id(0) == 0) & (pl.program_id(1) == 0))
    def _():
        def gen(a, carry):
            rows = pl.ds(pl.multiple_of(a * rb, rb), rb)
            c2, s2 = t2c_ref[pl.ds(a, 1), :], t2s_ref[pl.ds(a, 1), :]
            c1, s1 = t1c_ref[...], t1s_ref[...]
            cl_s[rows, :] = (c2 * c1 - s2 * s1).astype(BF16)
            sl_s[rows, :] = (s2 * c1 + c2 * s1).astype(BF16)
            return carry

        lax.fori_loop(0, t2c_ref.shape[0], gen, 0)

    mixed = (_dot(cl_s[...], zc_ref[...]) - _dot(sl_s[...], zs_ref[...])) * scale
    y = _dot(mixed.astype(BF16), fw_ref[...]) + fb_ref[...]
    o_ref[...] = (y * _silu(g_ref[...].astype(F32))).astype(o_ref.dtype)


def _fnet(p2, p3, fw, fb, *, u_col0, g_col0, tm):
    t = p2.shape[0]
    bsz, length, _ = p3.shape
    groups, gd, _ = fw.shape
    branch = groups * gd
    tm = min(tm, t)
    cc, sc = _dft_cos_sin(gd)
    rb = FNET_TABLE_ROWS
    t1c, t1s = _dft_rows(jnp.arange(rb, dtype=jnp.int32), length)
    t2c, t2s = _dft_rows(jnp.arange(length // rb, dtype=jnp.int32) * rb, length)
    const2 = lambda shape: pl.BlockSpec(shape, lambda i: (0, 0))
    zc, zs = pl.pallas_call(
        functools.partial(_fnet_cols_kernel, group=gd),
        grid=(t // tm,),
        in_specs=[pl.BlockSpec((tm, branch), lambda i: (i, u_col0 // branch)),
                  const2((gd, gd)), const2((gd, gd))],
        out_specs=[pl.BlockSpec((tm, branch), lambda i: (i, 0))] * 2,
        out_shape=[jax.ShapeDtypeStruct((t, branch), BF16)] * 2,
        compiler_params=_params("parallel"), name="fnet_cols",
    )(p2, cc, sc)
    zc = zc.reshape(bsz, length, branch)
    zs = zs.reshape(bsz, length, branch)
    gc0 = g_col0 // gd
    seq = lambda c0: pl.BlockSpec((None, length, gd), lambda b, g: (b, 0, c0 + g))
    table = lambda rows: pl.BlockSpec((rows, length), lambda b, g: (0, 0))
    return pl.pallas_call(
        functools.partial(_fnet_rows_kernel, scale=float((length * gd) ** -0.5)),
        grid=(bsz, groups),
        in_specs=[table(rb), table(rb), table(length // rb), table(length // rb), seq(0), seq(0),
                  pl.BlockSpec((None, gd, gd), lambda b, g: (g, 0, 0)),
                  pl.BlockSpec((1, gd), lambda b, g: (0, g)),
                  seq(gc0)],
        out_specs=seq(0),
        out_shape=jax.ShapeDtypeStruct((bsz, length, branch), BF16),
        scratch_shapes=[pltpu.VMEM((length, length), BF16), pltpu.VMEM((length, length), BF16)],
        compiler_params=_params("arbitrary", "arbitrary"), name="fnet_rows",
    )(t1c, t1s, t2c, t2s, zc, zs, fw, fb, p3)


def _conv3_rows(load_rows, r0, n_rows, length, w):
    h = BF16_SUBLANES
    u = load_rows(r0, n_rows)
    prev_blk = load_rows(pl.multiple_of(jnp.maximum(r0 - h, 0), h), h)
    next_blk = load_rows(pl.multiple_of(jnp.minimum(r0 + n_rows, length - h), h), h)
    prev_row = jnp.where(r0 > 0, prev_blk[h - 1:h, :], 0.0)
    next_row = jnp.where(r0 + n_rows < length, next_blk[0:1, :], 0.0)
    ri = lax.broadcasted_iota(jnp.int32, u.shape, 0)
    up = jnp.where(ri == 0, prev_row, pltpu.roll(u, 1, 0))
    un = jnp.where(ri == n_rows - 1, next_row, pltpu.roll(u, n_rows - 1, 0))
    return up * w[0:1, :] + u * w[1:2, :] + un * w[2:3, :]


def _sconv_kernel(cin_ref, cb_ref, cc_ref, cg_ref, w_ref, o_ref, *, rows_per_step):
    length, width = o_ref.shape

    def body(r, carry):
        r0 = pl.multiple_of(r * rows_per_step, rows_per_step)
        rows = pl.ds(r0, rows_per_step)
        for j in range(width // LANES):
            cols = slice(j * LANES, (j + 1) * LANES)

            def load_u(start, size, cols=cols):
                rs = pl.ds(start, size)
                return cc_ref[rs, cols].astype(F32) * cin_ref[rs, cols].astype(F32)

            conv = _conv3_rows(load_u, r0, rows_per_step, length, w_ref[:, cols])
            o_ref[rows, cols] = (cb_ref[rows, cols].astype(F32) * conv
                                 * _silu(cg_ref[rows, cols].astype(F32))).astype(o_ref.dtype)
        return carry

    lax.fori_loop(0, length // rows_per_step, body, 0)


def _sconv(p3, w, *, tn):
    bsz, length, _ = p3.shape
    branch = w.shape[1]
    nb = branch // tn
    seq = lambda k: pl.BlockSpec((None, length, tn), lambda b, j, k=k: (b, 0, k * nb + j))
    return pl.pallas_call(
        functools.partial(_sconv_kernel, rows_per_step=min(128, length)),
        grid=(bsz, nb),
        in_specs=[seq(0), seq(1), seq(2), seq(3), pl.BlockSpec((3, tn), lambda b, j: (0, j))],
        out_specs=seq(0),
        out_shape=jax.ShapeDtypeStruct((bsz, length, branch), BF16),
        compiler_params=_params("parallel", "parallel"), name="sconv",
    )(p3, p3, p3, p3, w)


def _ssd_kernel(x_ref, bm_ref, cm_ref, z_ref, dt_ref,
                wx_ref, wb_ref, wc_ref, bx_ref, bb_ref, bc_ref,
                dtb_ref, a_ref, dsk_ref,
                o_ref,
                bmt_s, cm_s, yacc_s, ecol_s, xdte_s, alast_s, st_s,
                *staging, chunk):
    length, gw = yacc_s.shape
    n_chunks = length // chunk
    heads = SSD_GROUP_HEADS
    hd = gw // heads
    xs_t, ae_t, de_t, ac_t, at_t, cb_t = zip(staging[:6], staging[6:])
    halo = BF16_SUBLANES
    ext = chunk + 2 * halo
    edge = (chunk - 1, 0)

    ri = lax.broadcasted_iota(jnp.int32, (chunk, chunk), 0)
    ci = lax.broadcasted_iota(jnp.int32, (chunk, chunk), 1)
    masks = (ri >= ci, ri <= ci)
    tris = []
    for m in masks:
        tri = jnp.where(m, 1.0, 0.0).astype(BF16)
        tris.append(jnp.concatenate([tri, tri], axis=1))
    low = lax.broadcasted_iota(jnp.int32, (chunk, LANES), 1).astype(F32).astype(BF16) < hd
    si = lax.broadcasted_iota(jnp.int32, (chunk, ext), 0)
    sj = lax.broadcasted_iota(jnp.int32, (chunk, ext), 1)
    sel_l = lax.broadcasted_iota(jnp.int32, (LANES, gw), 0)
    sel_c = lax.broadcasted_iota(jnp.int32, (LANES, gw), 1) // hd
    expands = []
    for d in (0, 1):
        e = jnp.where(sel_l == sel_c + d * heads, 1.0, 0.0).astype(BF16)
        expands.append(jnp.concatenate([e, e], axis=0))
    a2_row = a_ref[...] * LOG2E

    n_pairs = gw // LANES

    def front_steps(c, slot):
        env = {}

        def conv_silu(ref, w_ref, b_ref, cols):
            both = _dot(env["shift"], ref[env["erows"], cols])
            w = w_ref[:, cols]
            y = (both[:chunk] * w[0:1, :] + ref[env["rows"], cols].astype(F32) * w[1:2, :]
                 + both[chunk:] * w[2:3, :] + b_ref[:, cols])
            return _silu(y)

        def b_conv():
            r0 = pl.multiple_of(c * chunk, chunk)
            env["rows"] = pl.ds(r0, chunk)
            start = pl.multiple_of(jnp.clip(r0 - halo, 0, length - ext), halo)
            off = r0 - start
            env["erows"] = pl.ds(start, ext)
            env["shift"] = jnp.concatenate([jnp.where(sj == si + (off - 1), 1.0, 0.0),
                                            jnp.where(sj == si + (off + 1), 1.0, 0.0)], axis=0).astype(BF16)
            env["bmt"] = bmt_c = conv_silu(bm_ref, wb_ref, bb_ref, slice(0, LANES)).T.astype(BF16)
            bmt_s[c] = bmt_c

        def c_conv():
            cm_c = conv_silu(cm_ref, wc_ref, bc_ref, slice(0, LANES)).astype(BF16)
            cm_s[env["rows"], :] = cm_c
            cb_t[slot][...] = _dot(cm_c, env["bmt"])

        def decay(d):
            if d == 0:
                t = dt_ref[env["rows"], :] + dtb_ref[...]
                env["dtc"] = jnp.maximum(t, 0.0) + jnp.log1p(jnp.exp(-jnp.abs(t)))
            a = _sel_dot_rhs(tris[d], env["dtc"] * a2_row)
            ac_t[slot][d] = a
            at_t[slot][d] = a.T

        def spread(d):
            ae_t[slot][d] = _dot(jnp.concatenate(_split2(ac_t[slot][d]), axis=1), expands[d])
            de_t[slot][d] = _dot(jnp.concatenate(_split2(env["dtc"]), axis=1), expands[d])

        def x_conv(s):
            wide = gw // 2
            wcols = slice(s * wide, (s + 1) * wide)
            xs_t[slot][:, wcols] = conv_silu(x_ref, wx_ref, bx_ref, wcols)

        part = functools.partial
        return [b_conv, c_conv, part(decay, 0), part(decay, 1), part(spread, 0), part(spread, 1),
                part(x_conv, 0), part(x_conv, 1)]

    def back_unit(c, slot, p, d, acc):
        rows = pl.ds(pl.multiple_of(c * chunk, chunk), chunk)
        slab = slice(p * LANES, (p + 1) * LANES)
        xs_p = xs_t[slot][:, slab]
        if d == 0:
            acc = xs_p * dsk_ref[:, slab]
        l0 = d * heads + 2 * p
        acum_p = ae_t[slot][d, :, slab]
        xdt = xs_p * de_t[slot][d, :, slab]
        a_edge = acum_p[edge[d]:edge[d] + 1, :]
        ecol_s[d, rows, slab] = jnp.exp2(acum_p).astype(BF16)
        xdte_s[d, rows, slab] = (xdt * jnp.exp2(a_edge - acum_p)).astype(BF16)
        alast_s[d * n_chunks + c, :, slab] = jnp.broadcast_to(jnp.exp2(a_edge), (8, LANES))
        ms = []
        for l in (l0, l0 + 1):
            diff = ac_t[slot][d, :, l:l + 1] - at_t[slot][d, l:l + 1, :]
            ms.append((cb_t[slot][...] * jnp.exp2(jnp.where(masks[d], diff, -1e30))).astype(BF16))
        xb = xdt.astype(BF16)
        zero = jnp.zeros_like(xb)
        block_diag = jnp.concatenate([jnp.where(low, xb, zero), jnp.where(low, zero, xb)], axis=0)
        acc = acc + _dot(jnp.concatenate(ms, axis=1), block_diag)
        if d == 1:
            yacc_s[rows, slab] = acc
        return acc

    def prepare_two(i, carry):
        c0 = 2 * i
        for c_back, s_back, c_front, s_front in ((c0, 0, c0 + 1, 1),
                                                 (c0 + 1, 1, jnp.minimum(c0 + 2, n_chunks - 1), 0)):
            steps = front_steps(c_front, s_front)
            acc = None
            for p in range(n_pairs):
                for d in (0, 1):
                    acc = back_unit(c_back, s_back, p, d, acc)
                    steps[2 * p + d]()
        return carry

    for step in front_steps(0, 0):
        step()
    lax.fori_loop(0, n_chunks // 2, prepare_two, 0)

    st_s[...] = jnp.zeros_like(st_s)

    def scan(c, carry, *, finalize):
        for d in (0, 1):
            cc = c if d == 0 else n_chunks - 1 - c
            rows = pl.ds(pl.multiple_of(cc * chunk, chunk), chunk)
            state = st_s[d]
            y = _dot(cm_s[rows, :], state.astype(BF16)) * ecol_s[d, rows, :].astype(F32)
            st_s[d] = state * alast_s[d * n_chunks + cc][0:1, :] + _dot(bmt_s[cc], xdte_s[d, rows, :])
            if finalize:
                tot = yacc_s[rows, :] + y
                o_ref[rows, :] = (tot * _silu(z_ref[rows, :].astype(F32))).astype(o_ref.dtype)
            else:
                yacc_s[rows, :] = yacc_s[rows, :] + y
        return carry

    half = n_chunks // 2
    lax.fori_loop(0, half, functools.partial(scan, finalize=False), 0)
    lax.fori_loop(half, n_chunks, functools.partial(scan, finalize=True), 0)


def _ssd(p3, dt3, conv_w, conv_b, dt_bias_g, a_g, dsk, *, x_col0, z_col0, chunk):
    bsz, length, _ = p3.shape
    groups = dt3.shape[2] // LANES
    gw, ns = SSD_GROUP_W, SSD_STATE
    branch = groups * gw
    xb, bb, cb = x_col0 // gw, (x_col0 + branch) // ns, (x_col0 + branch + groups * ns) // ns
    seq = lambda w, c0: pl.BlockSpec((None, length, w), lambda b, g: (b, 0, c0 + g))
    par = lambda rows, w, c0: pl.BlockSpec((rows, w), lambda b, g: (0, c0 + g))
    n_chunks = length // chunk
    return pl.pallas_call(
        functools.partial(_ssd_kernel, chunk=chunk),
        grid=(bsz, groups),
        in_specs=[seq(gw, xb), seq(ns, bb), seq(ns, cb), seq(gw, z_col0 // gw), seq(LANES, 0),
                  par(3, gw, 0), par(3, ns, branch // ns), par(3, ns, (branch + groups * ns) // ns),
                  par(1, gw, 0), par(1, ns, branch // ns), par(1, ns, (branch + groups * ns) // ns),
                  par(1, LANES, 0), par(1, LANES, 0),
                  par(1, gw, 0)],
        out_specs=seq(gw, 0),
        out_shape=jax.ShapeDtypeStruct((bsz, length, branch), BF16),
        scratch_shapes=[pltpu.VMEM((n_chunks, ns, chunk), BF16),
                        pltpu.VMEM((length, ns), BF16),
                        pltpu.VMEM((length, gw), F32),
                        pltpu.VMEM((2, length, gw), BF16),
                        pltpu.VMEM((2, length, gw), BF16),
                        pltpu.VMEM((2 * n_chunks, 8, gw), F32),
                        pltpu.VMEM((2, ns, gw), F32),
                        ] + 2 * [
                        pltpu.VMEM((chunk, gw), F32),
                        pltpu.VMEM((2, chunk, gw), F32),
                        pltpu.VMEM((2, chunk, gw), F32),
                        pltpu.VMEM((2, chunk, LANES), F32),
                        pltpu.VMEM((2, LANES, chunk), F32),
                        pltpu.VMEM((chunk, chunk), F32)],
        compiler_params=_params("parallel", "parallel"), name="ssd",
    )(p3, p3, p3, p3, dt3, conv_w, conv_w, conv_w, conv_b, conv_b, conv_b,
      dt_bias_g, a_g, dsk)


def _ssd_group_lanes(v, groups):
    hpg = v.shape[1] // groups
    per = jnp.concatenate([v[0].reshape(groups, hpg), v[1].reshape(groups, hpg)], axis=1)
    per = jnp.pad(per, ((0, 0), (0, LANES - 2 * hpg)))
    return per.reshape(1, groups * LANES)


def kernel(x, norm_w, final_norm_w, ev_w_in, ev_w_out, hgrn_lb_logits, hgrn_norm_w, fnet_w, fnet_b,
           od_w_in, od_w_out, sconv_w, ssd_conv_w, ssd_conv_b, ssd_dt_bias, ssd_a_log, ssd_d, ssd_norm_w):
    bsz, length, d = x.shape
    t = bsz * length
    branch = d
    x2 = x.reshape(t, d)
    row = lambda v: v.astype(F32).reshape(1, -1)

    lower_bounds = jnp.cumsum(jax.nn.softmax(hgrn_lb_logits.astype(F32), axis=0), axis=0)
    p = _inproj(x2, row(norm_w[0]), ev_w_in[0].astype(BF16), n=ev_w_in.shape[2], tm=1024, tn=1024)
    p3 = p.reshape(bsz, length, -1)
    a_out = _hgrn(p3, row(lower_bounds[0]), row(hgrn_norm_w[0]), col0=0, chunk=HGRN_CHUNK)
    b_out = _fnet(p, p3, fnet_w[0].astype(BF16), row(fnet_b[0]),
                  u_col0=5 * branch, g_col0=6 * branch, tm=512)
    x2 = _outproj(a_out.reshape(t, branch), b_out.reshape(t, branch), x2,
                  ev_w_out[0].astype(BF16), tm=512)

    heads = ssd_d.shape[1]
    groups = heads // SSD_GROUP_HEADS
    n_main = 5 * branch + branch + 2 * groups * SSD_STATE
    w_in = od_w_in[0]
    w_dt = w_in[:, n_main:].reshape(d, 2, groups, SSD_GROUP_HEADS)
    w_dt = jnp.transpose(w_dt, (0, 2, 1, 3)).reshape(d, groups, 2 * SSD_GROUP_HEADS)
    w_dt = jnp.pad(w_dt, ((0, 0), (0, 0), (0, LANES - 2 * SSD_GROUP_HEADS))).reshape(d, groups * LANES)
    p, dt = _inproj(x2, row(norm_w[1]), w_in[:, :n_main].astype(BF16), w_dt.astype(BF16),
                    n=n_main, tm=1024, tn=1024)
    p3 = p.reshape(bsz, length, n_main)
    c_out = _sconv(p3, sconv_w[0].astype(F32), tn=512)
    a_neg = -jnp.exp(ssd_a_log[0].astype(F32))
    d_out = _ssd(p3, dt.reshape(bsz, length, groups * LANES),
                 ssd_conv_w[0].astype(F32), row(ssd_conv_b[0]),
                 _ssd_group_lanes(ssd_dt_bias[0].astype(F32), groups),
                 _ssd_group_lanes(a_neg, groups),
                 row(jnp.repeat(ssd_d[0].astype(F32), SSD_HEAD)),
                 x_col0=5 * branch, z_col0=4 * branch, chunk=SSD_CHUNK)
    out = _outproj(c_out.reshape(t, branch), d_out.reshape(t, branch), x2,
                   od_w_out[0].astype(BF16), row(ssd_norm_w[0]), row(final_norm_w), tm=512)
    return out.reshape(bsz, length, d)
```

```python
import functools

import jax
import jax.numpy as jnp
from jax import lax
from jax.experimental import pallas as pl
from jax.experimental.pallas import tpu as pltpu

F32 = jnp.float32
BF16 = jnp.bfloat16
EPS = 1e-6
LOG2E = 1.4426950408889634

VMEM_LIMIT_BYTES = 56 * 1024 * 1024
LANES = 128
BF16_SUBLANES = 16

HGRN_HEAD = 128
HGRN_CHUNK = 64
FNET_GROUP = 256
FNET_TABLE_ROWS = 64
SSD_HEAD = 64
SSD_STATE = 128
SSD_GROUP_HEADS = 8
SSD_GROUP_W = SSD_HEAD * SSD_GROUP_HEADS
SSD_CHUNK = 128


def _params(*sem):
    return pltpu.CompilerParams(dimension_semantics=sem, vmem_limit_bytes=VMEM_LIMIT_BYTES)


def _dot(a, b):
    return jnp.dot(a, b, preferred_element_type=F32)


def _dot_nt(a, b):
    return lax.dot_general(a, b, (((1,), (1,)), ((), ())), preferred_element_type=F32)


def _dot_tn(a, b):
    return lax.dot_general(a, b, (((0,), (0,)), ((), ())), preferred_element_type=F32)


def _split2(x):
    hi = x.astype(BF16)
    lo = (x - hi.astype(F32)).astype(BF16)
    return hi, lo


def _sel_dot_rhs(sel_sel, x):
    return _dot(sel_sel, jnp.concatenate(_split2(x), axis=0))


def _sel_dot_lhs(x, sel_sel):
    return _dot(jnp.concatenate(_split2(x), axis=1), sel_sel)


def _silu(x):
    return x * jax.nn.sigmoid(x)


def _rms_scale(x):
    return lax.rsqrt(jnp.mean(x * x, axis=-1, keepdims=True) + EPS)


def _cast_kernel(w_ref, o_ref):
    o_ref[...] = w_ref[...].astype(o_ref.dtype)


def _cast_cols_bf16(w, n, *, tn):
    d = w.shape[0]
    return pl.pallas_call(
        _cast_kernel, grid=(n // tn,),
        in_specs=[pl.BlockSpec((d, tn), lambda j: (0, j))],
        out_specs=pl.BlockSpec((d, tn), lambda j: (0, j)),
        out_shape=jax.ShapeDtypeStruct((d, n), BF16),
        compiler_params=_params("parallel"), name="cast_cols",
    )(w)


def _norm_rows_to(h_ref, x_ref, nw_ref, rows_per_step):
    nw = nw_ref[...]

    def body(r, carry):
        rows = pl.ds(pl.multiple_of(r * rows_per_step, rows_per_step), rows_per_step)
        x = x_ref[rows, :]
        h_ref[rows, :] = (x * _rms_scale(x) * nw).astype(BF16)
        return carry

    lax.fori_loop(0, x_ref.shape[0] // rows_per_step, body, 0)


def _inproj_kernel(x_ref, nw_ref, w_ref, o_ref, h_ref):
    @pl.when(pl.program_id(1) == 0)
    def _():
        _norm_rows_to(h_ref, x_ref, nw_ref, 128)

    o_ref[...] = _dot(h_ref[...], w_ref[...]).astype(o_ref.dtype)


def _inproj_dt_kernel(x_ref, nw_ref, w_ref, wdt_ref, o_ref, dt_ref, h_ref):
    @pl.when(pl.program_id(1) == 0)
    def _():
        _norm_rows_to(h_ref, x_ref, nw_ref, 128)
        dt_ref[...] = _dot(h_ref[...], wdt_ref[...])

    o_ref[...] = _dot(h_ref[...], w_ref[...]).astype(o_ref.dtype)


def _inproj(x2, nw, w, wdt=None, *, n, tm, tn):
    t, d = x2.shape
    tm = min(tm, t)
    grid = (t // tm, n // tn)
    in_specs = [
        pl.BlockSpec((tm, d), lambda i, j: (i, 0)),
        pl.BlockSpec((1, d), lambda i, j: (0, 0)),
        pl.BlockSpec((d, tn), lambda i, j: (0, j)),
    ]
    out_main = pl.BlockSpec((tm, tn), lambda i, j: (i, j))
    scratch = [pltpu.VMEM((tm, d), BF16)]
    if wdt is None:
        return pl.pallas_call(
            _inproj_kernel, grid=grid, in_specs=in_specs, out_specs=out_main,
            out_shape=jax.ShapeDtypeStruct((t, n), BF16), scratch_shapes=scratch,
            compiler_params=_params("parallel", "arbitrary"), name="inproj",
        )(x2, nw, w)
    ndt = wdt.shape[1]
    return pl.pallas_call(
        _inproj_dt_kernel, grid=grid,
        in_specs=in_specs + [pl.BlockSpec((d, ndt), lambda i, j: (0, 0))],
        out_specs=[out_main, pl.BlockSpec((tm, ndt), lambda i, j: (i, 0))],
        out_shape=[jax.ShapeDtypeStruct((t, n), BF16), jax.ShapeDtypeStruct((t, ndt), F32)],
        scratch_shapes=scratch,
        compiler_params=_params("parallel", "arbitrary"), name="inproj_dt",
    )(x2, nw, w, wdt)


def _outproj_kernel(*refs, norm_b, final_norm):
    a_ref, b_ref, x_ref, w_ref = refs[:4]
    rest = list(refs[4:])
    o_ref = rest.pop()
    ka = a_ref.shape[1]
    b = b_ref[...]
    if norm_b:
        bnw_ref = rest.pop(0)
        bf = b.astype(F32)
        b = (bf * _rms_scale(bf) * bnw_ref[...]).astype(BF16)
    acc = _dot(a_ref[...], w_ref[0:ka, :]) + _dot(b, w_ref[ka:, :]) + x_ref[...]
    if final_norm:
        fnw_ref = rest.pop(0)
        acc = acc * _rms_scale(acc) * fnw_ref[...]
    o_ref[...] = acc


def _outproj(a, b, x2, w, b_norm_w=None, final_norm_w=None, *, tm):
    t, d = x2.shape
    ka, kb = a.shape[1], b.shape[1]
    tm = min(tm, t)
    row = lambda width: pl.BlockSpec((tm, width), lambda i: (i, 0))
    const = lambda shape: pl.BlockSpec(shape, lambda i: (0, 0))
    in_specs = [row(ka), row(kb), row(d), const((ka + kb, d))]
    args = [a, b, x2, w]
    if b_norm_w is not None:
        in_specs.append(const((1, kb)))
        args.append(b_norm_w)
    if final_norm_w is not None:
        in_specs.append(const((1, d)))
        args.append(final_norm_w)
    kern = functools.partial(_outproj_kernel, norm_b=b_norm_w is not None,
                             final_norm=final_norm_w is not None)
    return pl.pallas_call(
        kern, grid=(t // tm,), in_specs=in_specs, out_specs=row(d),
        out_shape=jax.ShapeDtypeStruct((t, d), F32),
        compiler_params=_params("parallel"), name="outproj",
    )(*args)


def _hgrn_kernel(q_ref, v_ref, zf_ref, zb_ref, g_ref, lb_ref, nw_ref, o_ref,
                 qin_s, kin_s, qdec_s, kendt_s, v_s, dec_s, acc_s, *, chunk, block, n_heads):
    length, width = acc_s.shape
    n_chunks = length // chunk
    n_blocks = length // block
    per_block = block // chunk
    mid = chunk // 2
    z_refs = (zf_ref, zb_ref)
    edge = (chunk - 1, 0)

    def prepare(blk, st):
        lb = lb_ref[...]
        one_m_lb = 1.0 - lb
        bri = lax.broadcasted_iota(jnp.int32, (block, block), 0)
        bci = lax.broadcasted_iota(jnp.int32, (block, block), 1)
        same_chunk = (bri // chunk) == (bci // chunk)
        r0 = pl.multiple_of(blk * block, block)
        rows = pl.ds(r0, block)
        q = q_ref[rows, :].astype(F32)
        v_s[st, rows, :] = v_ref[rows, :]
        for d in (0, 1):
            tri = jnp.where(same_chunk & ((bri <= bci) if d else (bri >= bci)), 1.0, 0.0).astype(BF16)
            z = z_refs[d][rows, :].astype(F32)
            sg = jax.nn.sigmoid(z)
            logf = jnp.log2(lb + one_m_lb * sg)
            key = one_m_lb * (1.0 - sg)
            b2 = _dot(tri, jnp.concatenate(_split2(logf), axis=1))
            b = b2[:, :width] + b2[:, width:]
            for k in range(per_block):
                sl = slice(k * chunk, (k + 1) * chunk)
                rk = pl.ds(r0 + k * chunk, chunk)
                bk = b[sl]
                b_mid = bk[mid:mid + 1, :]
                b_edge = bk[edge[d]:edge[d] + 1, :]
                q_in = q[sl] * jnp.exp2(bk - b_mid)
                k_in = key[sl] * jnp.exp2(b_mid - bk)
                qin_s[st, d, rk, :] = q_in.astype(BF16)
                kin_s[st, d, rk, :] = k_in.astype(BF16)
                qdec_s[st, d, rk, :] = (q_in * jnp.exp2(b_mid)).astype(BF16)
                idx = d * n_chunks + blk * per_block + k
                kendt_s[st, idx] = (k_in * jnp.exp2(b_edge - b_mid)).T.astype(BF16)
                dec_s[st, idx] = jnp.broadcast_to(jnp.exp2(b_edge), (width, width)).T

    def scan(c, states, st, finalize):
        ri = lax.broadcasted_iota(jnp.int32, (chunk, chunk), 0)
        ci = lax.broadcasted_iota(jnp.int32, (chunk, chunk), 1)
        new_states = []
        for d in (0, 1):
            cc = c if d == 0 else n_chunks - 1 - c
            rows = pl.ds(pl.multiple_of(cc * chunk, chunk), chunk)
            idx = d * n_chunks + cc
            v = v_s[st, rows, :]
            s = _dot_nt(qin_s[st, d, rows, :], kin_s[st, d, rows, :])
            s = jnp.where((ri <= ci) if d else (ri >= ci), s, 0.0).astype(BF16)
            o = _dot(jnp.concatenate([qdec_s[st, d, rows, :], s], axis=1),
                     jnp.concatenate([states[d].astype(BF16), v], axis=0))
            new_states.append(states[d] * dec_s[st, idx] + _dot(kendt_s[st, idx], v))
            if finalize:
                tot = acc_s[rows, :] + o
                g = g_ref[rows, :].astype(F32)
                o_ref[rows, :] = (tot * _rms_scale(tot) * nw_ref[...] * _silu(g)).astype(o_ref.dtype)
            else:
                acc_s[rows, :] = o
        return tuple(new_states)

    half = n_chunks // 2
    unroll = min(8, half)
    trips = half // unroll
    blocks_per_trip = n_blocks // (2 * trips)

    def run(prep_set, scan_set):
        def trip(i, states, *, finalize):
            if scan_set is not None:
                for u in range(unroll):
                    states = scan(i * unroll + u, states, scan_set, finalize)
            if prep_set is not None:
                for u in range(blocks_per_trip):
                    prepare(i * blocks_per_trip + u, prep_set)
            return states

        zero = jnp.zeros((width, width), F32)
        states = lax.fori_loop(0, trips, functools.partial(trip, finalize=False), (zero, zero))
        lax.fori_loop(trips, 2 * trips, functools.partial(trip, finalize=True), states)

    step = pl.program_id(0)
    inner = (step > 0) & (step < n_heads)
    pl.when(step == 0)(lambda: run(0, None))
    pl.when(inner & (step % 2 == 1))(lambda: run(1, 0))
    pl.when(inner & (step % 2 == 0))(lambda: run(0, 1))
    pl.when(step == n_heads)(lambda: run(None, (n_heads - 1) % 2))


def _hgrn(p3, lb, nw, *, col0, chunk):
    bsz, length, _ = p3.shape
    width = HGRN_HEAD
    branch = lb.shape[1]
    heads = branch // width
    n_heads = bsz * heads
    c0 = col0 // width
    n_chunks = length // chunk
    cur = lambda s: jnp.minimum(s, n_heads - 1)
    prev = lambda s: jnp.maximum(s - 1, 0)
    seq = lambda k, at: pl.BlockSpec((None, length, width),
                                     lambda s: (at(s) // heads, 0, c0 + k * heads + at(s) % heads))
    vec = lambda at: pl.BlockSpec((1, width), lambda s: (0, at(s) % heads))
    scaled = pltpu.VMEM((2, 2, length, width), BF16)
    return pl.pallas_call(
        functools.partial(_hgrn_kernel, chunk=chunk, block=min(128, length), n_heads=n_heads),
        grid=(n_heads + 1,),
        in_specs=[seq(0, cur), seq(1, cur), seq(2, cur), seq(3, cur), seq(4, prev), vec(cur), vec(prev)],
        out_specs=pl.BlockSpec((None, length, width), lambda s: (prev(s) // heads, 0, prev(s) % heads)),
        out_shape=jax.ShapeDtypeStruct((bsz, length, branch), BF16),
        scratch_shapes=[scaled, scaled, scaled,
                        pltpu.VMEM((2, 2 * n_chunks, width, chunk), BF16),
                        pltpu.VMEM((2, length, width), BF16),
                        pltpu.VMEM((2, 2 * n_chunks, width, width), F32),
                        pltpu.VMEM((length, width), F32)],
        compiler_params=_params("arbitrary"), name="hgrn2",
    )(p3, p3, p3, p3, p3, lb, nw)


def _dft_cos_sin(n):
    k = lax.broadcasted_iota(jnp.int32, (n, n), 0) * lax.broadcasted_iota(jnp.int32, (n, n), 1)
    ang = (k % n).astype(F32) * (2.0 * jnp.pi / n)
    return jnp.cos(ang).astype(BF16), jnp.sin(ang).astype(BF16)


def _dft_rows(rows, n):
    k = rows[:, None] * lax.broadcasted_iota(jnp.int32, (rows.shape[0], n), 1)
    ang = (k % n).astype(F32) * (2.0 * jnp.pi / n)
    return jnp.cos(ang), jnp.sin(ang)


def _fnet_kernel(t1c_ref, t1s_ref, t2c_ref, t2s_ref, ccsc_ref, u_ref, fw_ref, fb_ref, g_ref, o_ref,
                 cs_s, *, scale):
    rb = t1c_ref.shape[0]
    length, gd = u_ref.shape

    @pl.when((pl.program_id(0) == 0) & (pl.program_id(1) == 0))
    def _():
        def gen(a, carry):
            rows = pl.ds(pl.multiple_of(a * rb, rb), rb)
            c2, s2 = t2c_ref[pl.ds(a, 1), :], t2s_ref[pl.ds(a, 1), :]
            c1, s1 = t1c_ref[...], t1s_ref[...]
            cs_s[rows, :length] = (c2 * c1 - s2 * s1).astype(BF16)
            cs_s[rows, length:] = (-(s2 * c1 + c2 * s1)).astype(BF16)
            return carry

        lax.fori_loop(0, t2c_ref.shape[0], gen, 0)

    z = _dot(u_ref[...], ccsc_ref[...])
    zz = jnp.concatenate([z[:, :gd], z[:, gd:]], axis=0).astype(BF16)
    mixed = _dot(cs_s[...], zz) * scale
    y = _dot(mixed.astype(BF16), fw_ref[...]) + fb_ref[...]
    o_ref[...] = (y * _silu(g_ref[...].astype(F32))).astype(o_ref.dtype)


def _fnet(p3, fw, fb, *, u_col0, g_col0):
    bsz, length, _ = p3.shape
    groups, gd, _ = fw.shape
    branch = groups * gd
    ccsc = jnp.concatenate(_dft_cos_sin(gd), axis=1)
    rb = FNET_TABLE_ROWS
    t1c, t1s = _dft_rows(jnp.arange(rb, dtype=jnp.int32), length)
    t2c, t2s = _dft_rows(jnp.arange(length // rb, dtype=jnp.int32) * rb, length)
    seq = lambda c0: pl.BlockSpec((None, length, gd), lambda b, g: (b, 0, c0 + g))
    table = lambda rows: pl.BlockSpec((rows, length), lambda b, g: (0, 0))
    return pl.pallas_call(
        functools.partial(_fnet_kernel, scale=float((length * gd) ** -0.5)),
        grid=(bsz, groups),
        in_specs=[table(rb), table(rb), table(length // rb), table(length // rb),
                  pl.BlockSpec((gd, 2 * gd), lambda b, g: (0, 0)),
                  seq(u_col0 // gd),
                  pl.BlockSpec((None, gd, gd), lambda b, g: (g, 0, 0)),
                  pl.BlockSpec((1, gd), lambda b, g: (0, g)),
                  seq(g_col0 // gd)],
        out_specs=seq(0),
        out_shape=jax.ShapeDtypeStruct((bsz, length, branch), BF16),
        scratch_shapes=[pltpu.VMEM((length, 2 * length), BF16)],
        compiler_params=_params("arbitrary", "arbitrary"), name="fnet",
    )(t1c, t1s, t2c, t2s, ccsc, p3, fw, fb, p3)


def _conv3_rows(load_rows, r0, n_rows, length, w):
    h = BF16_SUBLANES
    u = load_rows(r0, n_rows)
    prev_blk = load_rows(pl.multiple_of(jnp.maximum(r0 - h, 0), h), h)
    next_blk = load_rows(pl.multiple_of(jnp.minimum(r0 + n_rows, length - h), h), h)
    prev_row = jnp.where(r0 > 0, prev_blk[h - 1:h, :], 0.0)
    next_row = jnp.where(r0 + n_rows < length, next_blk[0:1, :], 0.0)
    ri = lax.broadcasted_iota(jnp.int32, u.shape, 0)
    up = jnp.where(ri == 0, prev_row, pltpu.roll(u, 1, 0))
    un = jnp.where(ri == n_rows - 1, next_row, pltpu.roll(u, n_rows - 1, 0))
    return up * w[0:1, :] + u * w[1:2, :] + un * w[2:3, :]


def _sconv_kernel(cin_ref, cb_ref, cc_ref, cg_ref, w_ref, o_ref, *, rows_per_step):
    length, width = o_ref.shape

    def body(r, carry):
        r0 = pl.multiple_of(r * rows_per_step, rows_per_step)
        rows = pl.ds(r0, rows_per_step)
        for j in range(width // LANES):
            cols = slice(j * LANES, (j + 1) * LANES)

            def load_u(start, size, cols=cols):
                rs = pl.ds(start, size)
                return cc_ref[rs, cols].astype(F32) * cin_ref[rs, cols].astype(F32)

            conv = _conv3_rows(load_u, r0, rows_per_step, length, w_ref[:, cols])
            o_ref[rows, cols] = (cb_ref[rows, cols].astype(F32) * conv
                                 * _silu(cg_ref[rows, cols].astype(F32))).astype(o_ref.dtype)
        return carry

    lax.fori_loop(0, length // rows_per_step, body, 0)


def _sconv(p3, w, *, tn):
    bsz, length, _ = p3.shape
    branch = w.shape[1]
    nb = branch // tn
    seq = lambda k: pl.BlockSpec((None, length, tn), lambda b, j, k=k: (b, 0, k * nb + j))
    return pl.pallas_call(
        functools.partial(_sconv_kernel, rows_per_step=min(128, length)),
        grid=(bsz, nb),
        in_specs=[seq(0), seq(1), seq(2), seq(3), pl.BlockSpec((3, tn), lambda b, j: (0, j))],
        out_specs=seq(0),
        out_shape=jax.ShapeDtypeStruct((bsz, length, branch), BF16),
        compiler_params=_params("parallel", "parallel"), name="sconv",
    )(p3, p3, p3, p3, w)


def _ssd_kernel(x_ref, bm_ref, cm_ref, z_ref, dt_ref,
                wx_ref, wb_ref, wc_ref, bx_ref, bb_ref, bc_ref,
                dtb_ref, a_ref, dsk_ref,
                o_ref,
                bmt_s, cm_s, yacc_s, ecol_s, xdte_s, alast_s, st_s,
                *staging, chunk):
    length, gw = yacc_s.shape
    n_chunks = length // chunk
    heads = SSD_GROUP_HEADS
    hd = gw // heads
    xs_t, ae_t, de_t, ac_t, at_t, cb_t = zip(staging[:6], staging[6:])
    halo = BF16_SUBLANES
    ext = chunk + 2 * halo
    edge = (chunk - 1, 0)

    ri = lax.broadcasted_iota(jnp.int32, (chunk, chunk), 0)
    ci = lax.broadcasted_iota(jnp.int32, (chunk, chunk), 1)
    masks = (ri >= ci, ri <= ci)
    tris = []
    for m in masks:
        tri = jnp.where(m, 1.0, 0.0).astype(BF16)
        tris.append(jnp.concatenate([tri, tri], axis=1))
    low = lax.broadcasted_iota(jnp.int32, (chunk, LANES), 1).astype(F32).astype(BF16) < hd
    si = lax.broadcasted_iota(jnp.int32, (chunk, ext), 0)
    sj = lax.broadcasted_iota(jnp.int32, (chunk, ext), 1)
    sel_l = lax.broadcasted_iota(jnp.int32, (LANES, gw), 0)
    sel_c = lax.broadcasted_iota(jnp.int32, (LANES, gw), 1) // hd
    expands = []
    for d in (0, 1):
        e = jnp.where(sel_l == sel_c + d * heads, 1.0, 0.0).astype(BF16)
        expands.append(jnp.concatenate([e, e], axis=0))
    a2_row = a_ref[...] * LOG2E

    n_pairs = gw // LANES

    def front_steps(c, slot):
        env = {}

        def conv_silu(ref, w_ref, b_ref, cols):
            both = _dot(env["shift"], ref[env["erows"], cols])
            w = w_ref[:, cols]
            y = (both[:chunk] * w[0:1, :] + ref[env["rows"], cols].astype(F32) * w[1:2, :]
                 + both[chunk:] * w[2:3, :] + b_ref[:, cols])
            return _silu(y)

        def b_conv():
            r0 = pl.multiple_of(c * chunk, chunk)
            env["rows"] = pl.ds(r0, chunk)
            start = pl.multiple_of(jnp.clip(r0 - halo, 0, length - ext), halo)
            off = r0 - start
            env["erows"] = pl.ds(start, ext)
            env["shift"] = jnp.concatenate([jnp.where(sj == si + (off - 1), 1.0, 0.0),
                                            jnp.where(sj == si + (off + 1), 1.0, 0.0)], axis=0).astype(BF16)
            env["bmt"] = bmt_c = conv_silu(bm_ref, wb_ref, bb_ref, slice(0, LANES)).T.astype(BF16)
            bmt_s[c] = bmt_c

        def c_conv():
            cm_c = conv_silu(cm_ref, wc_ref, bc_ref, slice(0, LANES)).astype(BF16)
            cm_s[env["rows"], :] = cm_c
            cb_t[slot][...] = _dot(cm_c, env["bmt"])

        def decay(d):
            if d == 0:
                t = dt_ref[env["rows"], :] + dtb_ref[...]
                env["dtc"] = jnp.maximum(t, 0.0) + jnp.log1p(jnp.exp(-jnp.abs(t)))
            a = _sel_dot_rhs(tris[d], env["dtc"] * a2_row)
            ac_t[slot][d] = a
            at_t[slot][d] = a.T

        def spread(d):
            ae_t[slot][d] = _dot(jnp.concatenate(_split2(ac_t[slot][d]), axis=1), expands[d])
            de_t[slot][d] = _dot(jnp.concatenate(_split2(env["dtc"]), axis=1), expands[d])

        def x_conv(s):
            wide = gw // 2
            wcols = slice(s * wide, (s + 1) * wide)
            xs_t[slot][:, wcols] = conv_silu(x_ref, wx_ref, bx_ref, wcols)

        part = functools.partial
        return [b_conv, c_conv, part(decay, 0), part(decay, 1), part(spread, 0), part(spread, 1),
                part(x_conv, 0), part(x_conv, 1)]

    def back_unit(c, slot, p, d, acc):
        rows = pl.ds(pl.multiple_of(c * chunk, chunk), chunk)
        slab = slice(p * LANES, (p + 1) * LANES)
        xs_p = xs_t[slot][:, slab]
        if d == 0:
            acc = xs_p * dsk_ref[:, slab]
        l0 = d * heads + 2 * p
        acum_p = ae_t[slot][d, :, slab]
        xdt = xs_p * de_t[slot][d, :, slab]
        a_edge = acum_p[edge[d]:edge[d] + 1, :]
        ecol_s[d, rows, slab] = jnp.exp2(acum_p).astype(BF16)
        xdte_s[d, rows, slab] = (xdt * jnp.exp2(a_edge - acum_p)).astype(BF16)
        alast_s[d * n_chunks + c, :, slab] = jnp.broadcast_to(jnp.exp2(a_edge), (8, LANES))
        ms = []
        for l in (l0, l0 + 1):
            diff = ac_t[slot][d, :, l:l + 1] - at_t[slot][d, l:l + 1, :]
            ms.append((cb_t[slot][...] * jnp.exp2(jnp.where(masks[d], diff, -1e30))).astype(BF16))
        xb = xdt.astype(BF16)
        zero = jnp.zeros_like(xb)
        block_diag = jnp.concatenate([jnp.where(low, xb, zero), jnp.where(low, zero, xb)], axis=0)
        acc = acc + _dot(jnp.concatenate(ms, axis=1), block_diag)
        if d == 1:
            yacc_s[rows, slab] = acc
        return acc

    def prepare_two(i, carry):
        c0 = 2 * i
        for c_back, s_back, c_front, s_front in ((c0, 0, c0 + 1, 1),
                                                 (c0 + 1, 1, jnp.minimum(c0 + 2, n_chunks - 1), 0)):
            steps = front_steps(c_front, s_front)
            acc = None
            for p in range(n_pairs):
                for d in (0, 1):
                    acc = back_unit(c_back, s_back, p, d, acc)
                    steps[2 * p + d]()
        return carry

    for step in front_steps(0, 0):
        step()
    lax.fori_loop(0, n_chunks // 2, prepare_two, 0)

    st_s[...] = jnp.zeros_like(st_s)

    def scan(c, carry, *, finalize):
        for d in (0, 1):
            cc = c if d == 0 else n_chunks - 1 - c
            rows = pl.ds(pl.multiple_of(cc * chunk, chunk), chunk)
            state = st_s[d]
            y = _dot(cm_s[rows, :], state.astype(BF16)) * ecol_s[d, rows, :].astype(F32)
            st_s[d] = state * alast_s[d * n_chunks + cc][0:1, :] + _dot(bmt_s[cc], xdte_s[d, rows, :])
            if finalize:
                tot = yacc_s[rows, :] + y
                o_ref[rows, :] = (tot * _silu(z_ref[rows, :].astype(F32))).astype(o_ref.dtype)
            else:
                yacc_s[rows, :] = yacc_s[rows, :] + y
        return carry

    half = n_chunks // 2
    lax.fori_loop(0, half, functools.partial(scan, finalize=False), 0)
    lax.fori_loop(half, n_chunks, functools.partial(scan, finalize=True), 0)


def _ssd(p3, dt3, conv_w, conv_b, dt_bias_g, a_g, dsk, *, x_col0, z_col0, chunk):
    bsz, length, _ = p3.shape
    groups = dt3.shape[2] // LANES
    gw, ns = SSD_GROUP_W, SSD_STATE
    branch = groups * gw
    xb, bb, cb = x_col0 // gw, (x_col0 + branch) // ns, (x_col0 + branch + groups * ns) // ns
    seq = lambda w, c0: pl.BlockSpec((None, length, w), lambda b, g: (b, 0, c0 + g))
    par = lambda rows, w, c0: pl.BlockSpec((rows, w), lambda b, g: (0, c0 + g))
    n_chunks = length // chunk
    return pl.pallas_call(
        functools.partial(_ssd_kernel, chunk=chunk),
        grid=(bsz, groups),
        in_specs=[seq(gw, xb), seq(ns, bb), seq(ns, cb), seq(gw, z_col0 // gw), seq(LANES, 0),
                  par(3, gw, 0), par(3, ns, branch // ns), par(3, ns, (branch + groups * ns) // ns),
                  par(1, gw, 0), par(1, ns, branch // ns), par(1, ns, (branch + groups * ns) // ns),
                  par(1, LANES, 0), par(1, LANES, 0),
                  par(1, gw, 0)],
        out_specs=seq(gw, 0),
        out_shape=jax.ShapeDtypeStruct((bsz, length, branch), BF16),
        scratch_shapes=[pltpu.VMEM((n_chunks, ns, chunk), BF16),
                        pltpu.VMEM((length, ns), BF16),
                        pltpu.VMEM((length, gw), F32),
                        pltpu.VMEM((2, length, gw), BF16),
                        pltpu.VMEM((2, length, gw), BF16),
                        pltpu.VMEM((2 * n_chunks, 8, gw), F32),
                        pltpu.VMEM((2, ns, gw), F32),
                        ] + 2 * [
                        pltpu.VMEM((chunk, gw), F32),
                        pltpu.VMEM((2, chunk, gw), F32),
                        pltpu.VMEM((2, chunk, gw), F32),
                        pltpu.VMEM((2, chunk, LANES), F32),
                        pltpu.VMEM((2, LANES, chunk), F32),
                        pltpu.VMEM((chunk, chunk), F32)],
        compiler_params=_params("parallel", "parallel"), name="ssd",
    )(p3, p3, p3, p3, dt3, conv_w, conv_w, conv_w, conv_b, conv_b, conv_b,
      dt_bias_g, a_g, dsk)


def _ssd_group_lanes(v, groups):
    hpg = v.shape[1] // groups
    per = jnp.concatenate([v[0].reshape(groups, hpg), v[1].reshape(groups, hpg)], axis=1)
    per = jnp.pad(per, ((0, 0), (0, LANES - 2 * hpg)))
    return per.reshape(1, groups * LANES)


def kernel(x, norm_w, final_norm_w, ev_w_in, ev_w_out, hgrn_lb_logits, hgrn_norm_w, fnet_w, fnet_b,
           od_w_in, od_w_out, sconv_w, ssd_conv_w, ssd_conv_b, ssd_dt_bias, ssd_a_log, ssd_d, ssd_norm_w):
    bsz, length, d = x.shape
    t = bsz * length
    branch = d
    x2 = x.reshape(t, d)
    row = lambda v: v.astype(F32).reshape(1, -1)

    lower_bounds = jnp.cumsum(jax.nn.softmax(hgrn_lb_logits.astype(F32), axis=0), axis=0)
    p = _inproj(x2, row(norm_w[0]), ev_w_in[0].astype(BF16), n=ev_w_in.shape[2], tm=1024, tn=2048)
    p3 = p.reshape(bsz, length, -1)
    a_out = _hgrn(p3, row(lower_bounds[0]), row(hgrn_norm_w[0]), col0=0, chunk=HGRN_CHUNK)
    b_out = _fnet(p3, fnet_w[0].astype(BF16), row(fnet_b[0]), u_col0=5 * branch, g_col0=6 * branch)
    x2 = _outproj(a_out.reshape(t, branch), b_out.reshape(t, branch), x2,
                  ev_w_out[0].astype(BF16), tm=512)

    heads = ssd_d.shape[1]
    groups = heads // SSD_GROUP_HEADS
    n_main = 5 * branch + branch + 2 * groups * SSD_STATE
    w_in = od_w_in[0]
    w_dt = w_in[:, n_main:].reshape(d, 2, groups, SSD_GROUP_HEADS)
    w_dt = jnp.transpose(w_dt, (0, 2, 1, 3)).reshape(d, groups, 2 * SSD_GROUP_HEADS)
    w_dt = jnp.pad(w_dt, ((0, 0), (0, 0), (0, LANES - 2 * SSD_GROUP_HEADS))).reshape(d, groups * LANES)
    p, dt = _inproj(x2, row(norm_w[1]), _cast_cols_bf16(w_in, n_main, tn=1024), w_dt.astype(BF16),
                    n=n_main, tm=1024, tn=1024)
    p3 = p.reshape(bsz, length, n_main)
    c_out = _sconv(p3, sconv_w[0].astype(F32), tn=512)
    a_neg = -jnp.exp(ssd_a_log[0].astype(F32))
    d_out = _ssd(p3, dt.reshape(bsz, length, groups * LANES),
                 ssd_conv_w[0].astype(F32), row(ssd_conv_b[0]),
                 _ssd_group_lanes(ssd_dt_bias[0].astype(F32), groups),
                 _ssd_group_lanes(a_neg, groups),
                 row(jnp.repeat(ssd_d[0].astype(F32), SSD_HEAD)),
                 x_col0=5 * branch, z_col0=4 * branch, chunk=SSD_CHUNK)
    out = _outproj(c_out.reshape(t, branch), d_out.reshape(t, branch), x2,
                   od_w_out[0].astype(BF16), row(ssd_norm_w[0]), row(final_norm_w), tm=512)
    return out.reshape(bsz, length, d)
```

```python
import functools

import jax
import jax.numpy as jnp
from jax import lax
from jax.experimental import pallas as pl
from jax.experimental.pallas import tpu as pltpu

F32 = jnp.float32
BF16 = jnp.bfloat16
EPS = 1e-6
LOG2E = 1.4426950408889634

VMEM_LIMIT_BYTES = 56 * 1024 * 1024
LANES = 128
BF16_SUBLANES = 16

HGRN_HEAD = 128
HGRN_CHUNK = 64
FNET_GROUP = 256
FNET_TABLE_ROWS = 64
FNET_GROUPS_PER_STEP = 1
SSD_HEAD = 64
SSD_STATE = 128
SSD_GROUP_HEADS = 8
SSD_GROUP_W = SSD_HEAD * SSD_GROUP_HEADS
SSD_CHUNK = 128


def _params(*sem):
    return pltpu.CompilerParams(dimension_semantics=sem, vmem_limit_bytes=VMEM_LIMIT_BYTES)


def _dot(a, b):
    return jnp.dot(a, b, preferred_element_type=F32)


def _dot_nt(a, b):
    return lax.dot_general(a, b, (((1,), (1,)), ((), ())), preferred_element_type=F32)


def _dot_tn(a, b):
    return lax.dot_general(a, b, (((0,), (0,)), ((), ())), preferred_element_type=F32)


def _split2(x):
    hi = x.astype(BF16)
    lo = (x - hi.astype(F32)).astype(BF16)
    return hi, lo


def _sel_dot_rhs(sel_sel, x):
    return _dot(sel_sel, jnp.concatenate(_split2(x), axis=0))


def _sel_dot_lhs(x, sel_sel):
    return _dot(jnp.concatenate(_split2(x), axis=1), sel_sel)


def _silu(x):
    return x * jax.nn.sigmoid(x)


def _rms_scale(x):
    return lax.rsqrt(jnp.mean(x * x, axis=-1, keepdims=True) + EPS)


def _norm_rows_to(h_ref, x_ref, nw_ref, rows_per_step):
    nw = nw_ref[...]

    def body(r, carry):
        rows = pl.ds(pl.multiple_of(r * rows_per_step, rows_per_step), rows_per_step)
        x = x_ref[rows, :]
        h_ref[rows, :] = (x * _rms_scale(x) * nw).astype(BF16)
        return carry

    lax.fori_loop(0, x_ref.shape[0] // rows_per_step, body, 0)


def _inproj_kernel(x_ref, nw_ref, w_ref, o_ref, h_ref):
    @pl.when(pl.program_id(1) == 0)
    def _():
        _norm_rows_to(h_ref, x_ref, nw_ref, 128)

    o_ref[...] = _dot(h_ref[...], w_ref[...]).astype(o_ref.dtype)


def _inproj_dt_kernel(x_ref, nw_ref, w_ref, wdt_ref, o_ref, dt_ref, h_ref):
    @pl.when(pl.program_id(1) == 0)
    def _():
        _norm_rows_to(h_ref, x_ref, nw_ref, 128)
        dt_ref[...] = _dot(h_ref[...], wdt_ref[...])

    o_ref[...] = _dot(h_ref[...], w_ref[...]).astype(o_ref.dtype)


def _inproj(x2, nw, w, wdt=None, *, n, tm, tn):
    t, d = x2.shape
    tm = min(tm, t)
    grid = (t // tm, n // tn)
    in_specs = [
        pl.BlockSpec((tm, d), lambda i, j: (i, 0)),
        pl.BlockSpec((1, d), lambda i, j: (0, 0)),
        pl.BlockSpec((d, tn), lambda i, j: (0, j)),
    ]
    out_main = pl.BlockSpec((tm, tn), lambda i, j: (i, j))
    scratch = [pltpu.VMEM((tm, d), BF16)]
    if wdt is None:
        return pl.pallas_call(
            _inproj_kernel, grid=grid, in_specs=in_specs, out_specs=out_main,
            out_shape=jax.ShapeDtypeStruct((t, n), BF16), scratch_shapes=scratch,
            compiler_params=_params("parallel", "arbitrary"), name="inproj",
        )(x2, nw, w)
    ndt = wdt.shape[1]
    return pl.pallas_call(
        _inproj_dt_kernel, grid=grid,
        in_specs=in_specs + [pl.BlockSpec((d, ndt), lambda i, j: (0, 0))],
        out_specs=[out_main, pl.BlockSpec((tm, ndt), lambda i, j: (i, 0))],
        out_shape=[jax.ShapeDtypeStruct((t, n), BF16), jax.ShapeDtypeStruct((t, ndt), F32)],
        scratch_shapes=scratch,
        compiler_params=_params("parallel", "arbitrary"), name="inproj_dt",
    )(x2, nw, w, wdt)


def _outproj_kernel(*refs, norm_b, final_norm):
    a_ref, b_ref, x_ref, w_ref = refs[:4]
    rest = list(refs[4:])
    o_ref = rest.pop()
    ka = a_ref.shape[1]
    b = b_ref[...]
    if norm_b:
        bnw_ref = rest.pop(0)
        bf = b.astype(F32)
        b = (bf * _rms_scale(bf) * bnw_ref[...]).astype(BF16)
    acc = _dot(a_ref[...], w_ref[0:ka, :]) + _dot(b, w_ref[ka:, :]) + x_ref[...]
    if final_norm:
        fnw_ref = rest.pop(0)
        acc = acc * _rms_scale(acc) * fnw_ref[...]
    o_ref[...] = acc


def _outproj(a, b, x2, w, b_norm_w=None, final_norm_w=None, *, tm):
    t, d = x2.shape
    ka, kb = a.shape[1], b.shape[1]
    tm = min(tm, t)
    row = lambda width: pl.BlockSpec((tm, width), lambda i: (i, 0))
    const = lambda shape: pl.BlockSpec(shape, lambda i: (0, 0))
    in_specs = [row(ka), row(kb), row(d), const((ka + kb, d))]
    args = [a, b, x2, w]
    if b_norm_w is not None:
        in_specs.append(const((1, kb)))
        args.append(b_norm_w)
    if final_norm_w is not None:
        in_specs.append(const((1, d)))
        args.append(final_norm_w)
    kern = functools.partial(_outproj_kernel, norm_b=b_norm_w is not None,
                             final_norm=final_norm_w is not None)
    return pl.pallas_call(
        kern, grid=(t // tm,), in_specs=in_specs, out_specs=row(d),
        out_shape=jax.ShapeDtypeStruct((t, d), F32),
        compiler_params=_params("parallel"), name="outproj",
    )(*args)


def _hgrn_kernel(q_ref, v_ref, zf_ref, zb_ref, g_ref, lb_ref, nw_ref, o_ref,
                 qin_s, kin_s, qdec_s, kendt_s, v_s, dec_s, acc_s, *, chunk, block, n_heads):
    length, width = acc_s.shape
    n_chunks = length // chunk
    n_blocks = length // block
    per_block = block // chunk
    mid = chunk // 2
    z_refs = (zf_ref, zb_ref)
    edge = (chunk - 1, 0)

    def prepare(blk, st):
        lb = lb_ref[...]
        one_m_lb = 1.0 - lb
        bri = lax.broadcasted_iota(jnp.int32, (block, block), 0)
        bci = lax.broadcasted_iota(jnp.int32, (block, block), 1)
        same_chunk = (bri // chunk) == (bci // chunk)
        r0 = pl.multiple_of(blk * block, block)
        rows = pl.ds(r0, block)
        q = q_ref[rows, :].astype(F32)
        v_s[st, rows, :] = v_ref[rows, :]
        for d in (0, 1):
            tri = jnp.where(same_chunk & ((bri <= bci) if d else (bri >= bci)), 1.0, 0.0).astype(BF16)
            z = z_refs[d][rows, :].astype(F32)
            sg = jax.nn.sigmoid(z)
            logf = jnp.log2(lb + one_m_lb * sg)
            key = one_m_lb * (1.0 - sg)
            b2 = _dot(tri, jnp.concatenate(_split2(logf), axis=1))
            b = b2[:, :width] + b2[:, width:]
            for k in range(per_block):
                sl = slice(k * chunk, (k + 1) * chunk)
                rk = pl.ds(r0 + k * chunk, chunk)
                bk = b[sl]
                b_mid = bk[mid:mid + 1, :]
                b_edge = bk[edge[d]:edge[d] + 1, :]
                q_in = q[sl] * jnp.exp2(bk - b_mid)
                k_in = key[sl] * jnp.exp2(b_mid - bk)
                qin_s[st, d, rk, :] = q_in.astype(BF16)
                kin_s[st, d, rk, :] = k_in.astype(BF16)
                qdec_s[st, d, rk, :] = (q_in * jnp.exp2(b_mid)).astype(BF16)
                idx = d * n_chunks + blk * per_block + k
                kendt_s[st, idx] = (k_in * jnp.exp2(b_edge - b_mid)).T.astype(BF16)
                dec_s[st, idx] = jnp.broadcast_to(jnp.exp2(b_edge), (width, width)).T

    def scan(c, states, st, finalize):
        ri = lax.broadcasted_iota(jnp.int32, (chunk, chunk), 0)
        ci = lax.broadcasted_iota(jnp.int32, (chunk, chunk), 1)
        new_states = []
        for d in (0, 1):
            cc = c if d == 0 else n_chunks - 1 - c
            rows = pl.ds(pl.multiple_of(cc * chunk, chunk), chunk)
            idx = d * n_chunks + cc
            v = v_s[st, rows, :]
            s = _dot_nt(qin_s[st, d, rows, :], kin_s[st, d, rows, :])
            s = jnp.where((ri <= ci) if d else (ri >= ci), s, 0.0).astype(BF16)
            o = _dot(jnp.concatenate([qdec_s[st, d, rows, :], s], axis=1),
                     jnp.concatenate([states[d].astype(BF16), v], axis=0))
            new_states.append(states[d] * dec_s[st, idx] + _dot(kendt_s[st, idx], v))
            if finalize:
                tot = acc_s[rows, :] + o
                g = g_ref[rows, :].astype(F32)
                o_ref[rows, :] = (tot * _rms_scale(tot) * nw_ref[...] * _silu(g)).astype(o_ref.dtype)
            else:
                acc_s[rows, :] = o
        return tuple(new_states)

    half = n_chunks // 2
    unroll = min(8, half)
    trips = half // unroll
    blocks_per_trip = n_blocks // (2 * trips)

    def run(prep_set, scan_set):
        def trip(i, states, *, finalize):
            if scan_set is not None:
                for u in range(unroll):
                    states = scan(i * unroll + u, states, scan_set, finalize)
            if prep_set is not None:
                for u in range(blocks_per_trip):
                    prepare(i * blocks_per_trip + u, prep_set)
            return states

        zero = jnp.zeros((width, width), F32)
        states = lax.fori_loop(0, trips, functools.partial(trip, finalize=False), (zero, zero))
        lax.fori_loop(trips, 2 * trips, functools.partial(trip, finalize=True), states)

    step = pl.program_id(0)
    inner = (step > 0) & (step < n_heads)
    pl.when(step == 0)(lambda: run(0, None))
    pl.when(inner & (step % 2 == 1))(lambda: run(1, 0))
    pl.when(inner & (step % 2 == 0))(lambda: run(0, 1))
    pl.when(step == n_heads)(lambda: run(None, (n_heads - 1) % 2))


def _hgrn(p3, lb, nw, *, col0, chunk):
    bsz, length, _ = p3.shape
    width = HGRN_HEAD
    branch = lb.shape[1]
    heads = branch // width
    n_heads = bsz * heads
    c0 = col0 // width
    n_chunks = length // chunk
    cur = lambda s: jnp.minimum(s, n_heads - 1)
    prev = lambda s: jnp.maximum(s - 1, 0)
    seq = lambda k, at: pl.BlockSpec((None, length, width),
                                     lambda s: (at(s) // heads, 0, c0 + k * heads + at(s) % heads))
    vec = lambda at: pl.BlockSpec((1, width), lambda s: (0, at(s) % heads))
    scaled = pltpu.VMEM((2, 2, length, width), BF16)
    return pl.pallas_call(
        functools.partial(_hgrn_kernel, chunk=chunk, block=min(128, length), n_heads=n_heads),
        grid=(n_heads + 1,),
        in_specs=[seq(0, cur), seq(1, cur), seq(2, cur), seq(3, cur), seq(4, prev), vec(cur), vec(prev)],
        out_specs=pl.BlockSpec((None, length, width), lambda s: (prev(s) // heads, 0, prev(s) % heads)),
        out_shape=jax.ShapeDtypeStruct((bsz, length, branch), BF16),
        scratch_shapes=[scaled, scaled, scaled,
                        pltpu.VMEM((2, 2 * n_chunks, width, chunk), BF16),
                        pltpu.VMEM((2, length, width), BF16),
                        pltpu.VMEM((2, 2 * n_chunks, width, width), F32),
                        pltpu.VMEM((length, width), F32)],
        compiler_params=_params("arbitrary"), name="hgrn2",
    )(p3, p3, p3, p3, p3, lb, nw)


def _dft_cos_sin(n):
    k = lax.broadcasted_iota(jnp.int32, (n, n), 0) * lax.broadcasted_iota(jnp.int32, (n, n), 1)
    ang = (k % n).astype(F32) * (2.0 * jnp.pi / n)
    return jnp.cos(ang).astype(BF16), jnp.sin(ang).astype(BF16)


def _dft_rows(rows, n):
    k = rows[:, None] * lax.broadcasted_iota(jnp.int32, (rows.shape[0], n), 1)
    ang = (k % n).astype(F32) * (2.0 * jnp.pi / n)
    return jnp.cos(ang), jnp.sin(ang)


def _fnet_kernel(t1c_ref, t1s_ref, t2c_ref, t2s_ref, ccsc_ref, u_ref, fw_ref, fb_ref, g_ref, o_ref,
                 cs_s, *, scale):
    rb = t1c_ref.shape[0]
    length = u_ref.shape[0]
    gd = fw_ref.shape[1]

    @pl.when((pl.program_id(0) == 0) & (pl.program_id(1) == 0))
    def _():
        def gen(a, carry):
            rows = pl.ds(pl.multiple_of(a * rb, rb), rb)
            c2, s2 = t2c_ref[pl.ds(a, 1), :], t2s_ref[pl.ds(a, 1), :]
            c1, s1 = t1c_ref[...], t1s_ref[...]
            cs_s[rows, :length] = (c2 * c1 - s2 * s1).astype(BF16)
            cs_s[rows, length:] = (-(s2 * c1 + c2 * s1)).astype(BF16)
            return carry

        lax.fori_loop(0, t2c_ref.shape[0], gen, 0)

    for k in range(fw_ref.shape[0]):
        cols = slice(k * gd, (k + 1) * gd)
        z = _dot(u_ref[:, cols], ccsc_ref[...])
        zz = jnp.concatenate([z[:, :gd], z[:, gd:]], axis=0).astype(BF16)
        mixed = _dot(cs_s[...], zz) * scale
        y = _dot(mixed.astype(BF16), fw_ref[k]) + fb_ref[:, cols]
        o_ref[:, cols] = (y * _silu(g_ref[:, cols].astype(F32))).astype(o_ref.dtype)


def _fnet(p3, fw, fb, *, u_col0, g_col0):
    bsz, length, _ = p3.shape
    groups, gd, _ = fw.shape
    branch = groups * gd
    ccsc = jnp.concatenate(_dft_cos_sin(gd), axis=1)
    rb = FNET_TABLE_ROWS
    t1c, t1s = _dft_rows(jnp.arange(rb, dtype=jnp.int32), length)
    t2c, t2s = _dft_rows(jnp.arange(length // rb, dtype=jnp.int32) * rb, length)
    gps = FNET_GROUPS_PER_STEP
    wide = gps * gd
    seq = lambda c0: pl.BlockSpec((None, length, wide), lambda b, g: (b, 0, c0 + g))
    table = lambda rows: pl.BlockSpec((rows, length), lambda b, g: (0, 0))
    return pl.pallas_call(
        functools.partial(_fnet_kernel, scale=float((length * gd) ** -0.5)),
        grid=(bsz, groups // gps),
        in_specs=[table(rb), table(rb), table(length // rb), table(length // rb),
                  pl.BlockSpec((gd, 2 * gd), lambda b, g: (0, 0)),
                  seq(u_col0 // wide),
                  pl.BlockSpec((gps, gd, gd), lambda b, g: (g, 0, 0)),
                  pl.BlockSpec((1, wide), lambda b, g: (0, g)),
                  seq(g_col0 // wide)],
        out_specs=seq(0),
        out_shape=jax.ShapeDtypeStruct((bsz, length, branch), BF16),
        scratch_shapes=[pltpu.VMEM((length, 2 * length), BF16)],
        compiler_params=_params("arbitrary", "arbitrary"), name="fnet",
    )(t1c, t1s, t2c, t2s, ccsc, p3, fw, fb, p3)


def _conv3_rows(load_rows, r0, n_rows, length, w):
    h = BF16_SUBLANES
    u = load_rows(r0, n_rows)
    prev_blk = load_rows(pl.multiple_of(jnp.maximum(r0 - h, 0), h), h)
    next_blk = load_rows(pl.multiple_of(jnp.minimum(r0 + n_rows, length - h), h), h)
    prev_row = jnp.where(r0 > 0, prev_blk[h - 1:h, :], 0.0)
    next_row = jnp.where(r0 + n_rows < length, next_blk[0:1, :], 0.0)
    ri = lax.broadcasted_iota(jnp.int32, u.shape, 0)
    up = jnp.where(ri == 0, prev_row, pltpu.roll(u, 1, 0))
    un = jnp.where(ri == n_rows - 1, next_row, pltpu.roll(u, n_rows - 1, 0))
    return up * w[0:1, :] + u * w[1:2, :] + un * w[2:3, :]


def _sconv_kernel(cin_ref, cb_ref, cc_ref, cg_ref, w_ref, o_ref, *, rows_per_step):
    length, width = o_ref.shape

    def body(r, carry):
        r0 = pl.multiple_of(r * rows_per_step, rows_per_step)
        rows = pl.ds(r0, rows_per_step)
        for j in range(width // LANES):
            cols = slice(j * LANES, (j + 1) * LANES)

            def load_u(start, size, cols=cols):
                rs = pl.ds(start, size)
                return cc_ref[rs, cols].astype(F32) * cin_ref[rs, cols].astype(F32)

            conv = _conv3_rows(load_u, r0, rows_per_step, length, w_ref[:, cols])
            o_ref[rows, cols] = (cb_ref[rows, cols].astype(F32) * conv
                                 * _silu(cg_ref[rows, cols].astype(F32))).astype(o_ref.dtype)
        return carry

    lax.fori_loop(0, length // rows_per_step, body, 0)


def _sconv(p3, w, *, tn):
    bsz, length, _ = p3.shape
    branch = w.shape[1]
    nb = branch // tn
    seq = lambda k: pl.BlockSpec((None, length, tn), lambda b, j, k=k: (b, 0, k * nb + j))
    return pl.pallas_call(
        functools.partial(_sconv_kernel, rows_per_step=min(128, length)),
        grid=(bsz, nb),
        in_specs=[seq(0), seq(1), seq(2), seq(3), pl.BlockSpec((3, tn), lambda b, j: (0, j))],
        out_specs=seq(0),
        out_shape=jax.ShapeDtypeStruct((bsz, length, branch), BF16),
        compiler_params=_params("parallel", "parallel"), name="sconv",
    )(p3, p3, p3, p3, w)


def _ssd_kernel(x_ref, bm_ref, cm_ref, z_ref, dt_ref,
                wx_ref, wb_ref, wc_ref, bx_ref, bb_ref, bc_ref,
                dtb_ref, a_ref, dsk_ref,
                o_ref,
                bmt_s, cm_s, yacc_s, ecol_s, xdte_s, alast_s, st_s,
                *staging, chunk):
    length, gw = yacc_s.shape
    n_chunks = length // chunk
    heads = SSD_GROUP_HEADS
    hd = gw // heads
    xs_t, ae_t, de_t, ac_t, at_t, cb_t = zip(staging[:6], staging[6:])
    halo = BF16_SUBLANES
    ext = chunk + 2 * halo
    edge = (chunk - 1, 0)

    ri = lax.broadcasted_iota(jnp.int32, (chunk, chunk), 0)
    ci = lax.broadcasted_iota(jnp.int32, (chunk, chunk), 1)
    masks = (ri >= ci, ri <= ci)
    tris = []
    for m in masks:
        tri = jnp.where(m, 1.0, 0.0).astype(BF16)
        tris.append(jnp.concatenate([tri, tri], axis=1))
    low = lax.broadcasted_iota(jnp.int32, (chunk, LANES), 1).astype(F32).astype(BF16) < hd
    si = lax.broadcasted_iota(jnp.int32, (chunk, ext), 0)
    sj = lax.broadcasted_iota(jnp.int32, (chunk, ext), 1)
    sel_l = lax.broadcasted_iota(jnp.int32, (LANES, gw), 0)
    sel_c = lax.broadcasted_iota(jnp.int32, (LANES, gw), 1) // hd
    expands = []
    for d in (0, 1):
        e = jnp.where(sel_l == sel_c + d * heads, 1.0, 0.0).astype(BF16)
        expands.append(jnp.concatenate([e, e], axis=0))
    a2_row = a_ref[...] * LOG2E

    n_pairs = gw // LANES

    def front_steps(c, slot):
        env = {}

        def conv_silu(ref, w_ref, b_ref, cols):
            both = _dot(env["shift"], ref[env["erows"], cols])
            w = w_ref[:, cols]
            y = (both[:chunk] * w[0:1, :] + ref[env["rows"], cols].astype(F32) * w[1:2, :]
                 + both[chunk:] * w[2:3, :] + b_ref[:, cols])
            return _silu(y)

        def b_conv():
            r0 = pl.multiple_of(c * chunk, chunk)
            env["rows"] = pl.ds(r0, chunk)
            start = pl.multiple_of(jnp.clip(r0 - halo, 0, length - ext), halo)
            off = r0 - start
            env["erows"] = pl.ds(start, ext)
            env["shift"] = jnp.concatenate([jnp.where(sj == si + (off - 1), 1.0, 0.0),
                                            jnp.where(sj == si + (off + 1), 1.0, 0.0)], axis=0).astype(BF16)
            env["bmt"] = bmt_c = conv_silu(bm_ref, wb_ref, bb_ref, slice(0, LANES)).T.astype(BF16)
            bmt_s[c] = bmt_c

        def c_conv():
            cm_c = conv_silu(cm_ref, wc_ref, bc_ref, slice(0, LANES)).astype(BF16)
            cm_s[env["rows"], :] = cm_c
            cb_t[slot][...] = _dot(cm_c, env["bmt"])

        def decay(d):
            if d == 0:
                to_front = (LANES - pl.program_id(1) * 2 * heads) % LANES
                t = pltpu.roll(dt_ref[env["rows"], :], to_front, 1) + dtb_ref[...]
                env["dtc"] = jnp.maximum(t, 0.0) + jnp.log1p(jnp.exp(-jnp.abs(t)))
            a = _sel_dot_rhs(tris[d], env["dtc"] * a2_row)
            ac_t[slot][d] = a
            at_t[slot][d] = a.T

        def spread(d):
            ae_t[slot][d] = _dot(jnp.concatenate(_split2(ac_t[slot][d]), axis=1), expands[d])
            de_t[slot][d] = _dot(jnp.concatenate(_split2(env["dtc"]), axis=1), expands[d])

        def x_conv(s):
            wide = gw // 2
            wcols = slice(s * wide, (s + 1) * wide)
            xs_t[slot][:, wcols] = conv_silu(x_ref, wx_ref, bx_ref, wcols)

        part = functools.partial
        return [b_conv, c_conv, part(decay, 0), part(decay, 1), part(spread, 0), part(spread, 1),
                part(x_conv, 0), part(x_conv, 1)]

    def back_unit(c, slot, p, d, acc):
        rows = pl.ds(pl.multiple_of(c * chunk, chunk), chunk)
        slab = slice(p * LANES, (p + 1) * LANES)
        xs_p = xs_t[slot][:, slab]
        if d == 0:
            acc = xs_p * dsk_ref[:, slab]
        l0 = d * heads + 2 * p
        acum_p = ae_t[slot][d, :, slab]
        xdt = xs_p * de_t[slot][d, :, slab]
        a_edge = acum_p[edge[d]:edge[d] + 1, :]
        ecol_s[d, rows, slab] = jnp.exp2(acum_p).astype(BF16)
        xdte_s[d, rows, slab] = (xdt * jnp.exp2(a_edge - acum_p)).astype(BF16)
        alast_s[d * n_chunks + c, :, slab] = jnp.broadcast_to(jnp.exp2(a_edge), (8, LANES))
        ms = []
        for l in (l0, l0 + 1):
            diff = ac_t[slot][d, :, l:l + 1] - at_t[slot][d, l:l + 1, :]
            ms.append((cb_t[slot][...] * jnp.exp2(jnp.where(masks[d], diff, -1e30))).astype(BF16))
        xb = xdt.astype(BF16)
        zero = jnp.zeros_like(xb)
        block_diag = jnp.concatenate([jnp.where(low, xb, zero), jnp.where(low, zero, xb)], axis=0)
        acc = acc + _dot(jnp.concatenate(ms, axis=1), block_diag)
        if d == 1:
            yacc_s[rows, slab] = acc
        return acc

    def prepare_two(i, carry):
        c0 = 2 * i
        for c_back, s_back, c_front, s_front in ((c0, 0, c0 + 1, 1),
                                                 (c0 + 1, 1, jnp.minimum(c0 + 2, n_chunks - 1), 0)):
            steps = front_steps(c_front, s_front)
            acc = None
            for p in range(n_pairs):
                for d in (0, 1):
                    acc = back_unit(c_back, s_back, p, d, acc)
                    steps[2 * p + d]()
        return carry

    for step in front_steps(0, 0):
        step()
    lax.fori_loop(0, n_chunks // 2, prepare_two, 0)

    st_s[...] = jnp.zeros_like(st_s)

    def scan(c, carry, *, finalize):
        for d in (0, 1):
            cc = c if d == 0 else n_chunks - 1 - c
            rows = pl.ds(pl.multiple_of(cc * chunk, chunk), chunk)
            state = st_s[d]
            y = _dot(cm_s[rows, :], state.astype(BF16)) * ecol_s[d, rows, :].astype(F32)
            st_s[d] = state * alast_s[d * n_chunks + cc][0:1, :] + _dot(bmt_s[cc], xdte_s[d, rows, :])
            if finalize:
                tot = yacc_s[rows, :] + y
                o_ref[rows, :] = (tot * _silu(z_ref[rows, :].astype(F32))).astype(o_ref.dtype)
            else:
                yacc_s[rows, :] = yacc_s[rows, :] + y
        return carry

    half = n_chunks // 2
    lax.fori_loop(0, half, functools.partial(scan, finalize=False), 0)
    lax.fori_loop(half, n_chunks, functools.partial(scan, finalize=True), 0)


def _ssd(p3, dt3, conv_w, conv_b, dt_bias_g, a_g, dsk, *, x_col0, z_col0, chunk):
    bsz, length, _ = p3.shape
    gw, ns = SSD_GROUP_W, SSD_STATE
    branch = dsk.shape[1]
    groups = branch // gw
    xb, bb, cb = x_col0 // gw, (x_col0 + branch) // ns, (x_col0 + branch + groups * ns) // ns
    seq = lambda w, c0: pl.BlockSpec((None, length, w), lambda b, g: (b, 0, c0 + g))
    par = lambda rows, w, c0: pl.BlockSpec((rows, w), lambda b, g: (0, c0 + g))
    n_chunks = length // chunk
    return pl.pallas_call(
        functools.partial(_ssd_kernel, chunk=chunk),
        grid=(bsz, groups),
        in_specs=[seq(gw, xb), seq(ns, bb), seq(ns, cb), seq(gw, z_col0 // gw),
                  pl.BlockSpec((None, length, LANES), lambda b, g: (b, 0, 0)),
                  par(3, gw, 0), par(3, ns, branch // ns), par(3, ns, (branch + groups * ns) // ns),
                  par(1, gw, 0), par(1, ns, branch // ns), par(1, ns, (branch + groups * ns) // ns),
                  par(1, LANES, 0), par(1, LANES, 0),
                  par(1, gw, 0)],
        out_specs=seq(gw, 0),
        out_shape=jax.ShapeDtypeStruct((bsz, length, branch), BF16),
        scratch_shapes=[pltpu.VMEM((n_chunks, ns, chunk), BF16),
                        pltpu.VMEM((length, ns), BF16),
                        pltpu.VMEM((length, gw), F32),
                        pltpu.VMEM((2, length, gw), BF16),
                        pltpu.VMEM((2, length, gw), BF16),
                        pltpu.VMEM((2 * n_chunks, 8, gw), F32),
                        pltpu.VMEM((2, ns, gw), F32),
                        ] + 2 * [
                        pltpu.VMEM((chunk, gw), F32),
                        pltpu.VMEM((2, chunk, gw), F32),
                        pltpu.VMEM((2, chunk, gw), F32),
                        pltpu.VMEM((2, chunk, LANES), F32),
                        pltpu.VMEM((2, LANES, chunk), F32),
                        pltpu.VMEM((chunk, chunk), F32)],
        compiler_params=_params("parallel", "parallel"), name="ssd",
    )(p3, p3, p3, p3, dt3, conv_w, conv_w, conv_w, conv_b, conv_b, conv_b,
      dt_bias_g, a_g, dsk)


def _ssd_group_lanes(v, groups):
    hpg = v.shape[1] // groups
    per = jnp.concatenate([v[0].reshape(groups, hpg), v[1].reshape(groups, hpg)], axis=1)
    per = jnp.pad(per, ((0, 0), (0, LANES - 2 * hpg)))
    return per.reshape(1, groups * LANES)


def kernel(x, norm_w, final_norm_w, ev_w_in, ev_w_out, hgrn_lb_logits, hgrn_norm_w, fnet_w, fnet_b,
           od_w_in, od_w_out, sconv_w, ssd_conv_w, ssd_conv_b, ssd_dt_bias, ssd_a_log, ssd_d, ssd_norm_w):
    bsz, length, d = x.shape
    t = bsz * length
    branch = d
    x2 = x.reshape(t, d)
    row = lambda v: v.astype(F32).reshape(1, -1)

    lower_bounds = jnp.cumsum(jax.nn.softmax(hgrn_lb_logits.astype(F32), axis=0), axis=0)
    p = _inproj(x2, row(norm_w[0]), ev_w_in[0].astype(BF16), n=ev_w_in.shape[2], tm=1024, tn=2048)
    p3 = p.reshape(bsz, length, -1)
    a_out = _hgrn(p3, row(lower_bounds[0]), row(hgrn_norm_w[0]), col0=0, chunk=HGRN_CHUNK)
    b_out = _fnet(p3, fnet_w[0].astype(BF16), row(fnet_b[0]), u_col0=5 * branch, g_col0=6 * branch)
    x2 = _outproj(a_out.reshape(t, branch), b_out.reshape(t, branch), x2,
                  ev_w_out[0].astype(BF16), tm=512)

    heads = ssd_d.shape[1]
    groups = heads // SSD_GROUP_HEADS
    n_main = 5 * branch + branch + 2 * groups * SSD_STATE
    w_in = od_w_in[0]
    w_dt = w_in[:, n_main:].reshape(d, 2, groups, SSD_GROUP_HEADS)
    w_dt = jnp.transpose(w_dt, (0, 2, 1, 3)).reshape(d, 2 * heads)
    w_dt = jnp.pad(w_dt, ((0, 0), (0, LANES - 2 * heads)))
    p, dt = _inproj(x2, row(norm_w[1]), w_in.astype(BF16), w_dt.astype(BF16), n=n_main, tm=1024, tn=1024)
    p3 = p.reshape(bsz, length, n_main)
    c_out = _sconv(p3, sconv_w[0].astype(F32), tn=512)
    a_neg = -jnp.exp(ssd_a_log[0].astype(F32))
    d_out = _ssd(p3, dt.reshape(bsz, length, LANES),
                 ssd_conv_w[0].astype(F32), row(ssd_conv_b[0]),
                 _ssd_group_lanes(ssd_dt_bias[0].astype(F32), groups),
                 _ssd_group_lanes(a_neg, groups),
                 row(jnp.repeat(ssd_d[0].astype(F32), SSD_HEAD)),
                 x_col0=5 * branch, z_col0=4 * branch, chunk=SSD_CHUNK)
    out = _outproj(c_out.reshape(t, branch), d_out.reshape(t, branch), x2,
                   od_w_out[0].astype(BF16), row(ssd_norm_w[0]), row(final_norm_w), tm=512)
    return out.reshape(bsz, length, d)
```

```python
import functools

import jax
import jax.numpy as jnp
from jax import lax
from jax.experimental import pallas as pl
from jax.experimental.pallas import tpu as pltpu

F32 = jnp.float32
BF16 = jnp.bfloat16
EPS = 1e-6
LOG2E = 1.4426950408889634

VMEM_LIMIT_BYTES = 56 * 1024 * 1024
LANES = 128
BF16_SUBLANES = 16

HGRN_HEAD = 128
HGRN_CHUNK = 64
FNET_GROUP = 256
FNET_TABLE_ROWS = 64
FNET_GROUPS_PER_STEP = 1
SSD_HEAD = 64
SSD_STATE = 128
SSD_GROUP_HEADS = 8
SSD_GROUP_W = SSD_HEAD * SSD_GROUP_HEADS
SSD_CHUNK = 128


def _params(*sem):
    return pltpu.CompilerParams(dimension_semantics=sem, vmem_limit_bytes=VMEM_LIMIT_BYTES)


def _dot(a, b):
    return jnp.dot(a, b, preferred_element_type=F32)


def _dot_nt(a, b):
    return lax.dot_general(a, b, (((1,), (1,)), ((), ())), preferred_element_type=F32)


def _dot_tn(a, b):
    return lax.dot_general(a, b, (((0,), (0,)), ((), ())), preferred_element_type=F32)


def _split2(x):
    hi = x.astype(BF16)
    lo = (x - hi.astype(F32)).astype(BF16)
    return hi, lo


def _sel_dot_rhs(sel_sel, x):
    return _dot(sel_sel, jnp.concatenate(_split2(x), axis=0))


def _sel_dot_lhs(x, sel_sel):
    return _dot(jnp.concatenate(_split2(x), axis=1), sel_sel)


def _silu(x):
    return x * jax.nn.sigmoid(x)


def _rms_scale(x):
    return lax.rsqrt(jnp.mean(x * x, axis=-1, keepdims=True) + EPS)


def _cast_t_kernel(wt_ref, o_ref):
    o_ref[...] = wt_ref[...].T.astype(o_ref.dtype)


def _cast_transposed(wt, n, *, tn):
    d = wt.shape[1]
    return pl.pallas_call(
        _cast_t_kernel, grid=(n // tn,),
        in_specs=[pl.BlockSpec((tn, d), lambda j: (j, 0))],
        out_specs=pl.BlockSpec((d, tn), lambda j: (0, j)),
        out_shape=jax.ShapeDtypeStruct((d, n), BF16),
        compiler_params=_params("parallel"), name="cast_t",
    )(wt)


def _norm_rows_to(h_ref, x_ref, nw_ref, rows_per_step):
    nw = nw_ref[...]

    def body(r, carry):
        rows = pl.ds(pl.multiple_of(r * rows_per_step, rows_per_step), rows_per_step)
        x = x_ref[rows, :]
        h_ref[rows, :] = (x * _rms_scale(x) * nw).astype(BF16)
        return carry

    lax.fori_loop(0, x_ref.shape[0] // rows_per_step, body, 0)


def _inproj_kernel(x_ref, nw_ref, w_ref, o_ref, h_ref):
    @pl.when(pl.program_id(1) == 0)
    def _():
        _norm_rows_to(h_ref, x_ref, nw_ref, 128)

    o_ref[...] = _dot(h_ref[...], w_ref[...]).astype(o_ref.dtype)


def _inproj_dt_kernel(x_ref, nw_ref, w_ref, wdt_ref, o_ref, dt_ref, h_ref):
    @pl.when(pl.program_id(1) == 0)
    def _():
        _norm_rows_to(h_ref, x_ref, nw_ref, 128)
        dt_ref[...] = _dot(h_ref[...], wdt_ref[...])

    o_ref[...] = _dot(h_ref[...], w_ref[...]).astype(o_ref.dtype)


def _inproj(x2, nw, w, wdt=None, *, col0=0, n, tm, tn):
    t, d = x2.shape
    tm = min(tm, t)
    grid = (t // tm, n // tn)
    in_specs = [
        pl.BlockSpec((tm, d), lambda i, j: (i, 0)),
        pl.BlockSpec((1, d), lambda i, j: (0, 0)),
        pl.BlockSpec((d, tn), lambda i, j: (0, col0 // tn + j)),
    ]
    out_main = pl.BlockSpec((tm, tn), lambda i, j: (i, j))
    scratch = [pltpu.VMEM((tm, d), BF16)]
    if wdt is None:
        return pl.pallas_call(
            _inproj_kernel, grid=grid, in_specs=in_specs, out_specs=out_main,
            out_shape=jax.ShapeDtypeStruct((t, n), BF16), scratch_shapes=scratch,
            compiler_params=_params("parallel", "arbitrary"), name="inproj",
        )(x2, nw, w)
    ndt = wdt.shape[1]
    return pl.pallas_call(
        _inproj_dt_kernel, grid=grid,
        in_specs=in_specs + [pl.BlockSpec((d, ndt), lambda i, j: (0, 0))],
        out_specs=[out_main, pl.BlockSpec((tm, ndt), lambda i, j: (i, 0))],
        out_shape=[jax.ShapeDtypeStruct((t, n), BF16), jax.ShapeDtypeStruct((t, ndt), F32)],
        scratch_shapes=scratch,
        compiler_params=_params("parallel", "arbitrary"), name="inproj_dt",
    )(x2, nw, w, wdt)


def _outproj_kernel(*refs, norm_b, final_norm):
    a_ref, b_ref, x_ref, w_ref = refs[:4]
    rest = list(refs[4:])
    o_ref = rest.pop()
    ka = a_ref.shape[1]
    b = b_ref[...]
    if norm_b:
        bnw_ref = rest.pop(0)
        bf = b.astype(F32)
        b = (bf * _rms_scale(bf) * bnw_ref[...]).astype(BF16)
    acc = _dot(a_ref[...], w_ref[0:ka, :]) + _dot(b, w_ref[ka:, :]) + x_ref[...]
    if final_norm:
        fnw_ref = rest.pop(0)
        acc = acc * _rms_scale(acc) * fnw_ref[...]
    o_ref[...] = acc


def _outproj(a, b, x2, w, b_norm_w=None, final_norm_w=None, *, tm):
    t, d = x2.shape
    ka, kb = a.shape[1], b.shape[1]
    tm = min(tm, t)
    row = lambda width: pl.BlockSpec((tm, width), lambda i: (i, 0))
    const = lambda shape: pl.BlockSpec(shape, lambda i: (0, 0))
    in_specs = [row(ka), row(kb), row(d), const((ka + kb, d))]
    args = [a, b, x2, w]
    if b_norm_w is not None:
        in_specs.append(const((1, kb)))
        args.append(b_norm_w)
    if final_norm_w is not None:
        in_specs.append(const((1, d)))
        args.append(final_norm_w)
    kern = functools.partial(_outproj_kernel, norm_b=b_norm_w is not None,
                             final_norm=final_norm_w is not None)
    return pl.pallas_call(
        kern, grid=(t // tm,), in_specs=in_specs, out_specs=row(d),
        out_shape=jax.ShapeDtypeStruct((t, d), F32),
        compiler_params=_params("parallel"), name="outproj",
    )(*args)


def _hgrn_kernel(q_ref, v_ref, zf_ref, zb_ref, g_ref, lb_ref, nw_ref, o_ref,
                 qin_s, kin_s, qdec_s, kendt_s, v_s, dec_s, acc_s, *, chunk, block, n_heads):
    length, width = acc_s.shape
    n_chunks = length // chunk
    n_blocks = length // block
    per_block = block // chunk
    mid = chunk // 2
    z_refs = (zf_ref, zb_ref)
    edge = (chunk - 1, 0)

    def prepare(blk, st):
        lb = lb_ref[...]
        one_m_lb = 1.0 - lb
        bri = lax.broadcasted_iota(jnp.int32, (block, block), 0)
        bci = lax.broadcasted_iota(jnp.int32, (block, block), 1)
        same_chunk = (bri // chunk) == (bci // chunk)
        r0 = pl.multiple_of(blk * block, block)
        rows = pl.ds(r0, block)
        q = q_ref[rows, :].astype(F32)
        v_s[st, rows, :] = v_ref[rows, :]
        for d in (0, 1):
            tri = jnp.where(same_chunk & ((bri <= bci) if d else (bri >= bci)), 1.0, 0.0).astype(BF16)
            z = z_refs[d][rows, :].astype(F32)
            sg = jax.nn.sigmoid(z)
            logf = jnp.log2(lb + one_m_lb * sg)
            key = one_m_lb * (1.0 - sg)
            b2 = _dot(tri, jnp.concatenate(_split2(logf), axis=1))
            b = b2[:, :width] + b2[:, width:]
            for k in range(per_block):
                sl = slice(k * chunk, (k + 1) * chunk)
                rk = pl.ds(r0 + k * chunk, chunk)
                bk = b[sl]
                b_mid = bk[mid:mid + 1, :]
                b_edge = bk[edge[d]:edge[d] + 1, :]
                q_in = q[sl] * jnp.exp2(bk - b_mid)
                k_in = key[sl] * jnp.exp2(b_mid - bk)
                qin_s[st, d, rk, :] = q_in.astype(BF16)
                kin_s[st, d, rk, :] = k_in.astype(BF16)
                qdec_s[st, d, rk, :] = (q_in * jnp.exp2(b_mid)).astype(BF16)
                idx = d * n_chunks + blk * per_block + k
                kendt_s[st, idx] = (k_in * jnp.exp2(b_edge - b_mid)).T.astype(BF16)
                dec_s[st, idx] = jnp.broadcast_to(jnp.exp2(b_edge), (width, width)).T

    def scan(c, states, st, finalize):
        ri = lax.broadcasted_iota(jnp.int32, (chunk, chunk), 0)
        ci = lax.broadcasted_iota(jnp.int32, (chunk, chunk), 1)
        new_states = []
        for d in (0, 1):
            cc = c if d == 0 else n_chunks - 1 - c
            rows = pl.ds(pl.multiple_of(cc * chunk, chunk), chunk)
            idx = d * n_chunks + cc
            v = v_s[st, rows, :]
            s = _dot_nt(qin_s[st, d, rows, :], kin_s[st, d, rows, :])
            s = jnp.where((ri <= ci) if d else (ri >= ci), s, 0.0).astype(BF16)
            o = _dot(jnp.concatenate([qdec_s[st, d, rows, :], s], axis=1),
                     jnp.concatenate([states[d].astype(BF16), v], axis=0))
            new_states.append(states[d] * dec_s[st, idx] + _dot(kendt_s[st, idx], v))
            if finalize:
                tot = acc_s[rows, :] + o
                g = g_ref[rows, :].astype(F32)
                o_ref[rows, :] = (tot * _rms_scale(tot) * nw_ref[...] * _silu(g)).astype(o_ref.dtype)
            else:
                acc_s[rows, :] = o
        return tuple(new_states)

    half = n_chunks // 2
    unroll = min(8, half)
    trips = half // unroll
    blocks_per_trip = n_blocks // (2 * trips)

    def run(prep_set, scan_set):
        def trip(i, states, *, finalize):
            if scan_set is not None:
                for u in range(unroll):
                    states = scan(i * unroll + u, states, scan_set, finalize)
            if prep_set is not None:
                for u in range(blocks_per_trip):
                    prepare(i * blocks_per_trip + u, prep_set)
            return states

        zero = jnp.zeros((width, width), F32)
        states = lax.fori_loop(0, trips, functools.partial(trip, finalize=False), (zero, zero))
        lax.fori_loop(trips, 2 * trips, functools.partial(trip, finalize=True), states)

    step = pl.program_id(0)
    inner = (step > 0) & (step < n_heads)
    pl.when(step == 0)(lambda: run(0, None))
    pl.when(inner & (step % 2 == 1))(lambda: run(1, 0))
    pl.when(inner & (step % 2 == 0))(lambda: run(0, 1))
    pl.when(step == n_heads)(lambda: run(None, (n_heads - 1) % 2))


def _hgrn(p3, lb, nw, *, col0, chunk):
    bsz, length, _ = p3.shape
    width = HGRN_HEAD
    branch = lb.shape[1]
    heads = branch // width
    n_heads = bsz * heads
    c0 = col0 // width
    n_chunks = length // chunk
    cur = lambda s: jnp.minimum(s, n_heads - 1)
    prev = lambda s: jnp.maximum(s - 1, 0)
    seq = lambda k, at: pl.BlockSpec((None, length, width),
                                     lambda s: (at(s) // heads, 0, c0 + k * heads + at(s) % heads))
    vec = lambda at: pl.BlockSpec((1, width), lambda s: (0, at(s) % heads))
    scaled = pltpu.VMEM((2, 2, length, width), BF16)
    return pl.pallas_call(
        functools.partial(_hgrn_kernel, chunk=chunk, block=min(128, length), n_heads=n_heads),
        grid=(n_heads + 1,),
        in_specs=[seq(0, cur), seq(1, cur), seq(2, cur), seq(3, cur), seq(4, prev), vec(cur), vec(prev)],
        out_specs=pl.BlockSpec((None, length, width), lambda s: (prev(s) // heads, 0, prev(s) % heads)),
        out_shape=jax.ShapeDtypeStruct((bsz, length, branch), BF16),
        scratch_shapes=[scaled, scaled, scaled,
                        pltpu.VMEM((2, 2 * n_chunks, width, chunk), BF16),
                        pltpu.VMEM((2, length, width), BF16),
                        pltpu.VMEM((2, 2 * n_chunks, width, width), F32),
                        pltpu.VMEM((length, width), F32)],
        compiler_params=_params("arbitrary"), name="hgrn2",
    )(p3, p3, p3, p3, p3, lb, nw)


def _dft_cos_sin(n):
    k = lax.broadcasted_iota(jnp.int32, (n, n), 0) * lax.broadcasted_iota(jnp.int32, (n, n), 1)
    ang = (k % n).astype(F32) * (2.0 * jnp.pi / n)
    return jnp.cos(ang).astype(BF16), jnp.sin(ang).astype(BF16)


def _dft_rows(rows, n):
    k = rows[:, None] * lax.broadcasted_iota(jnp.int32, (rows.shape[0], n), 1)
    ang = (k % n).astype(F32) * (2.0 * jnp.pi / n)
    return jnp.cos(ang), jnp.sin(ang)


def _fnet_kernel(t1c_ref, t1s_ref, t2c_ref, t2s_ref, ccsc_ref, u_ref, fw_ref, fb_ref, g_ref, o_ref,
                 cs_s, *, scale):
    rb = t1c_ref.shape[0]
    length = u_ref.shape[0]
    gd = fw_ref.shape[1]

    @pl.when((pl.program_id(0) == 0) & (pl.program_id(1) == 0))
    def _():
        def gen(a, carry):
            rows = pl.ds(pl.multiple_of(a * rb, rb), rb)
            c2, s2 = t2c_ref[pl.ds(a, 1), :], t2s_ref[pl.ds(a, 1), :]
            c1, s1 = t1c_ref[...], t1s_ref[...]
            cs_s[rows, :length] = (c2 * c1 - s2 * s1).astype(BF16)
            cs_s[rows, length:] = (-(s2 * c1 + c2 * s1)).astype(BF16)
            return carry

        lax.fori_loop(0, t2c_ref.shape[0], gen, 0)

    for k in range(fw_ref.shape[0]):
        cols = slice(k * gd, (k + 1) * gd)
        z = _dot(u_ref[:, cols], ccsc_ref[...])
        zz = jnp.concatenate([z[:, :gd], z[:, gd:]], axis=0).astype(BF16)
        mixed = _dot(cs_s[...], zz) * scale
        y = _dot(mixed.astype(BF16), fw_ref[k]) + fb_ref[:, cols]
        o_ref[:, cols] = (y * _silu(g_ref[:, cols].astype(F32))).astype(o_ref.dtype)


def _fnet(p3, fw, fb, *, u_col0, g_col0):
    bsz, length, _ = p3.shape
    groups, gd, _ = fw.shape
    branch = groups * gd
    ccsc = jnp.concatenate(_dft_cos_sin(gd), axis=1)
    rb = FNET_TABLE_ROWS
    t1c, t1s = _dft_rows(jnp.arange(rb, dtype=jnp.int32), length)
    t2c, t2s = _dft_rows(jnp.arange(length // rb, dtype=jnp.int32) * rb, length)
    gps = FNET_GROUPS_PER_STEP
    wide = gps * gd
    seq = lambda c0: pl.BlockSpec((None, length, wide), lambda b, g: (b, 0, c0 + g))
    table = lambda rows: pl.BlockSpec((rows, length), lambda b, g: (0, 0))
    return pl.pallas_call(
        functools.partial(_fnet_kernel, scale=float((length * gd) ** -0.5)),
        grid=(bsz, groups // gps),
        in_specs=[table(rb), table(rb), table(length // rb), table(length // rb),
                  pl.BlockSpec((gd, 2 * gd), lambda b, g: (0, 0)),
                  seq(u_col0 // wide),
                  pl.BlockSpec((gps, gd, gd), lambda b, g: (g, 0, 0)),
                  pl.BlockSpec((1, wide), lambda b, g: (0, g)),
                  seq(g_col0 // wide)],
        out_specs=seq(0),
        out_shape=jax.ShapeDtypeStruct((bsz, length, branch), BF16),
        scratch_shapes=[pltpu.VMEM((length, 2 * length), BF16)],
        compiler_params=_params("arbitrary", "arbitrary"), name="fnet",
    )(t1c, t1s, t2c, t2s, ccsc, p3, fw, fb, p3)


class _RowShift:
    def __init__(self, chunk, length):
        self.chunk, self.length, self.halo = chunk, length, BF16_SUBLANES
        self.ext = chunk + 2 * self.halo
        si = lax.broadcasted_iota(jnp.int32, (chunk, self.ext), 0)
        sj = lax.broadcasted_iota(jnp.int32, (chunk, self.ext), 1)
        self.mats = [jnp.concatenate([jnp.where(sj == si + (off - 1), 1.0, 0.0),
                                      jnp.where(sj == si + (off + 1), 1.0, 0.0)], axis=0).astype(BF16)
                     for off in (0, self.halo, 2 * self.halo)]

    def window(self, r0):
        start = pl.multiple_of(jnp.clip(r0 - self.halo, 0, self.length - self.ext), self.halo)
        off = r0 - start
        mat = jnp.where(off == self.halo, self.mats[1], jnp.where(off == 0, self.mats[0], self.mats[2]))
        return pl.ds(start, self.ext), mat


def _conv3_rows(load_rows, r0, n_rows, length, w):
    h = BF16_SUBLANES
    u = load_rows(r0, n_rows)
    prev_blk = load_rows(pl.multiple_of(jnp.maximum(r0 - h, 0), h), h)
    next_blk = load_rows(pl.multiple_of(jnp.minimum(r0 + n_rows, length - h), h), h)
    prev_row = jnp.where(r0 > 0, prev_blk[h - 1:h, :], 0.0)
    next_row = jnp.where(r0 + n_rows < length, next_blk[0:1, :], 0.0)
    ri = lax.broadcasted_iota(jnp.int32, u.shape, 0)
    up = jnp.where(ri == 0, prev_row, pltpu.roll(u, 1, 0))
    un = jnp.where(ri == n_rows - 1, next_row, pltpu.roll(u, n_rows - 1, 0))
    return up * w[0:1, :] + u * w[1:2, :] + un * w[2:3, :]


def _sconv_kernel(cin_ref, cb_ref, cc_ref, cg_ref, w_ref, o_ref, *, chunk):
    length, width = o_ref.shape

    def body(r, carry):
        r0 = pl.multiple_of(r * chunk, chunk)
        rows = pl.ds(r0, chunk)
        for j in range(width // LANES):
            cols = slice(j * LANES, (j + 1) * LANES)

            def load_u(start, size, cols=cols):
                rs = pl.ds(start, size)
                return cc_ref[rs, cols].astype(F32) * cin_ref[rs, cols].astype(F32)

            conv = _conv3_rows(load_u, r0, chunk, length, w_ref[:, cols])
            o_ref[rows, cols] = (cb_ref[rows, cols].astype(F32) * conv
                                 * _silu(cg_ref[rows, cols].astype(F32))).astype(o_ref.dtype)
        return carry

    lax.fori_loop(0, length // chunk, body, 0)


def _sconv(p3, w, *, tn):
    bsz, length, _ = p3.shape
    branch = w.shape[1]
    nb = branch // tn
    seq = lambda k: pl.BlockSpec((None, length, tn), lambda b, j, k=k: (b, 0, k * nb + j))
    return pl.pallas_call(
        functools.partial(_sconv_kernel, chunk=min(128, length)),
        grid=(bsz, nb),
        in_specs=[seq(0), seq(1), seq(2), seq(3), pl.BlockSpec((3, tn), lambda b, j: (0, j))],
        out_specs=seq(0),
        out_shape=jax.ShapeDtypeStruct((bsz, length, branch), BF16),
        compiler_params=_params("parallel", "parallel"), name="sconv",
    )(p3, p3, p3, p3, w)


def _ssd_kernel(x_ref, bm_ref, cm_ref, z_ref, dt_ref,
                wx_ref, wb_ref, wc_ref, bx_ref, bb_ref, bc_ref,
                dtb_ref, a_ref, dsk_ref,
                o_ref,
                bmt_s, cm_s, yacc_s, ecol_s, xdte_s, alast_s, st_s,
                *staging, chunk):
    length, gw = yacc_s.shape
    n_chunks = length // chunk
    heads = SSD_GROUP_HEADS
    hd = gw // heads
    xs_t, ae_t, de_t, ac_t, at_t, cb_t = zip(staging[:6], staging[6:])
    shifter = _RowShift(chunk, length)
    edge = (chunk - 1, 0)

    ri = lax.broadcasted_iota(jnp.int32, (chunk, chunk), 0)
    ci = lax.broadcasted_iota(jnp.int32, (chunk, chunk), 1)
    masks = (ri >= ci, ri <= ci)
    tris = []
    for m in masks:
        tri = jnp.where(m, 1.0, 0.0).astype(BF16)
        tris.append(jnp.concatenate([tri, tri], axis=1))
    low = lax.broadcasted_iota(jnp.int32, (chunk, LANES), 1).astype(F32).astype(BF16) < hd
    sel_l =lax.broadcasted_iota(jnp.int32, (LANES, gw), 0)
    sel_c = lax.broadcasted_iota(jnp.int32, (LANES, gw), 1) // hd
    expands = []
    for d in (0, 1):
        e = jnp.where(sel_l == sel_c + d * heads, 1.0, 0.0).astype(BF16)
        expands.append(jnp.concatenate([e, e], axis=0))
    a2_row = a_ref[...] * LOG2E

    n_pairs = gw // LANES

    def front_steps(c, slot):
        env = {}

        def conv_silu(ref, w_ref, b_ref, cols):
            both = _dot(env["shift"], ref[env["erows"], cols])
            w = w_ref[:, cols]
            y = (both[:chunk] * w[0:1, :] + ref[env["rows"], cols].astype(F32) * w[1:2, :]
                 + both[chunk:] * w[2:3, :] + b_ref[:, cols])
            return _silu(y)

        def b_conv():
            r0 = pl.multiple_of(c * chunk, chunk)
            env["rows"] = pl.ds(r0, chunk)
            env["erows"], env["shift"] = shifter.window(r0)
            env["bmt"] = bmt_c = conv_silu(bm_ref, wb_ref, bb_ref, slice(0, LANES)).T.astype(BF16)
            bmt_s[c] = bmt_c

        def c_conv():
            cm_c = conv_silu(cm_ref, wc_ref, bc_ref, slice(0, LANES)).astype(BF16)
            cm_s[env["rows"], :] = cm_c
            cb_t[slot][...] = _dot(cm_c, env["bmt"])

        def decay(d):
            if d == 0:
                to_front = (LANES - pl.program_id(1) * 2 * heads) % LANES
                t = pltpu.roll(dt_ref[env["rows"], :], to_front, 1) + dtb_ref[...]
                env["dtc"] = jnp.maximum(t, 0.0) + jnp.log1p(jnp.exp(-jnp.abs(t)))
            a = _sel_dot_rhs(tris[d], env["dtc"] * a2_row)
            ac_t[slot][d] = a
            at_t[slot][d] = a.T

        def spread(d):
            ae_t[slot][d] = _dot(jnp.concatenate(_split2(ac_t[slot][d]), axis=1), expands[d])
            de_t[slot][d] = _dot(jnp.concatenate(_split2(env["dtc"]), axis=1), expands[d])

        def x_conv(s):
            wide = gw // 2
            wcols = slice(s * wide, (s + 1) * wide)
            xs_t[slot][:, wcols] = conv_silu(x_ref, wx_ref, bx_ref, wcols)

        part = functools.partial
        return [b_conv, c_conv, part(decay, 0), part(decay, 1), part(spread, 0), part(spread, 1),
                part(x_conv, 0), part(x_conv, 1)]

    def back_unit(c, slot, p, d, acc):
        rows = pl.ds(pl.multiple_of(c * chunk, chunk), chunk)
        slab = slice(p * LANES, (p + 1) * LANES)
        xs_p = xs_t[slot][:, slab]
        if d == 0:
            acc = xs_p * dsk_ref[:, slab]
        l0 = d * heads + 2 * p
        acum_p = ae_t[slot][d, :, slab]
        xdt = xs_p * de_t[slot][d, :, slab]
        a_edge = acum_p[edge[d]:edge[d] + 1, :]
        ecol_s[d, rows, slab] = jnp.exp2(acum_p).astype(BF16)
        xdte_s[d, rows, slab] = (xdt * jnp.exp2(a_edge - acum_p)).astype(BF16)
        alast_s[d * n_chunks + c, :, slab] = jnp.broadcast_to(jnp.exp2(a_edge), (8, LANES))
        ms = []
        for l in (l0, l0 + 1):
            diff = ac_t[slot][d, :, l:l + 1] - at_t[slot][d, l:l + 1, :]
            ms.append((cb_t[slot][...] * jnp.exp2(jnp.where(masks[d], diff, -1e30))).astype(BF16))
        xb = xdt.astype(BF16)
        zero = jnp.zeros_like(xb)
        block_diag = jnp.concatenate([jnp.where(low, xb, zero), jnp.where(low, zero, xb)], axis=0)
        acc = acc + _dot(jnp.concatenate(ms, axis=1), block_diag)
        if d == 1:
            yacc_s[rows, slab] = acc
        return acc

    def prepare_two(i, carry):
        c0 = 2 * i
        for c_back, s_back, c_front, s_front in ((c0, 0, c0 + 1, 1),
                                                 (c0 + 1, 1, jnp.minimum(c0 + 2, n_chunks - 1), 0)):
            steps = front_steps(c_front, s_front)
            acc = None
            for p in range(n_pairs):
                for d in (0, 1):
                    acc = back_unit(c_back, s_back, p, d, acc)
                    steps[2 * p + d]()
        return carry

    for step in front_steps(0, 0):
        step()
    lax.fori_loop(0, n_chunks // 2, prepare_two, 0)

    st_s[...] = jnp.zeros_like(st_s)

    def scan(c, carry, *, finalize):
        for d in (0, 1):
            cc = c if d == 0 else n_chunks - 1 - c
            rows = pl.ds(pl.multiple_of(cc * chunk, chunk), chunk)
            state = st_s[d]
            y = _dot(cm_s[rows, :], state.astype(BF16)) * ecol_s[d, rows, :].astype(F32)
            st_s[d] = state * alast_s[d * n_chunks + cc][0:1, :] + _dot(bmt_s[cc], xdte_s[d, rows, :])
            if finalize:
                tot = yacc_s[rows, :] + y
                o_ref[rows, :] = (tot * _silu(z_ref[rows, :].astype(F32))).astype(o_ref.dtype)
            else:
                yacc_s[rows, :] = yacc_s[rows, :] + y
        return carry

    half = n_chunks // 2
    lax.fori_loop(0, half, functools.partial(scan, finalize=False), 0)
    lax.fori_loop(half, n_chunks, functools.partial(scan, finalize=True), 0)


def _ssd(p3, bc3, dt3, conv_w, conv_b, dt_bias_g, a_g, dsk, *, x_col0, z_col0, chunk):
    bsz, length, _ = p3.shape
    gw, ns = SSD_GROUP_W, SSD_STATE
    branch = dsk.shape[1]
    groups = branch // gw
    xb, bb, cb = x_col0 // gw, 0, groups
    seq =lambda w, c0: pl.BlockSpec((None, length, w), lambda b, g: (b, 0, c0 + g))
    par = lambda rows, w, c0: pl.BlockSpec((rows, w), lambda b, g: (0, c0 + g))
    n_chunks = length // chunk
    return pl.pallas_call(
        functools.partial(_ssd_kernel, chunk=chunk),
        grid=(bsz, groups),
        in_specs=[seq(gw, xb), seq(ns, bb), seq(ns, cb), seq(gw, z_col0 // gw),
                  pl.BlockSpec((None, length, LANES), lambda b, g: (b, 0, 0)),
                  par(3, gw, 0), par(3, ns, branch // ns), par(3, ns, (branch + groups * ns) // ns),
                  par(1, gw, 0), par(1, ns, branch // ns), par(1, ns, (branch + groups * ns) // ns),
                  par(1, LANES, 0), par(1, LANES, 0),
                  par(1, gw, 0)],
        out_specs=seq(gw, 0),
        out_shape=jax.ShapeDtypeStruct((bsz, length, branch), BF16),
        scratch_shapes=[pltpu.VMEM((n_chunks, ns, chunk), BF16),
                        pltpu.VMEM((length, ns), BF16),
                        pltpu.VMEM((length, gw), F32),
                        pltpu.VMEM((2, length, gw), BF16),
                        pltpu.VMEM((2, length, gw), BF16),
                        pltpu.VMEM((2 * n_chunks, 8, gw), F32),
                        pltpu.VMEM((2, ns, gw), F32),
                        ] + 2 * [
                        pltpu.VMEM((chunk, gw), F32),
                        pltpu.VMEM((2, chunk, gw), F32),
                        pltpu.VMEM((2, chunk, gw), F32),
                        pltpu.VMEM((2, chunk, LANES), F32),
                        pltpu.VMEM((2, LANES, chunk), F32),
                        pltpu.VMEM((chunk, chunk), F32)],
        compiler_params=_params("parallel", "parallel"), name="ssd",
    )(p3, bc3, bc3, p3, dt3, conv_w, conv_w, conv_w, conv_b, conv_b, conv_b,
      dt_bias_g, a_g, dsk)


def _ssd_group_lanes(v, groups):
    hpg = v.shape[1] // groups
    per = jnp.concatenate([v[0].reshape(groups, hpg), v[1].reshape(groups, hpg)], axis=1)
    per = jnp.pad(per, ((0, 0), (0, LANES - 2 * hpg)))
    return per.reshape(1, groups * LANES)


def kernel(x, norm_w, final_norm_w, ev_w_in, ev_w_out, hgrn_lb_logits, hgrn_norm_w, fnet_w, fnet_b,
           od_w_in, od_w_out, sconv_w, ssd_conv_w, ssd_conv_b, ssd_dt_bias, ssd_a_log, ssd_d, ssd_norm_w):
    bsz, length, d = x.shape
    t = bsz * length
    branch = d
    x2 = x.reshape(t, d)
    row = lambda v: v.astype(F32).reshape(1, -1)

    lower_bounds = jnp.cumsum(jax.nn.softmax(hgrn_lb_logits.astype(F32), axis=0), axis=0)
    p = _inproj(x2, row(norm_w[0]), ev_w_in[0].astype(BF16), n=ev_w_in.shape[2], tm=1024, tn=2048)
    p3 = p.reshape(bsz, length, -1)
    a_out = _hgrn(p3, row(lower_bounds[0]), row(hgrn_norm_w[0]), col0=0, chunk=HGRN_CHUNK)
    b_out = _fnet(p3, fnet_w[0].astype(BF16), row(fnet_b[0]), u_col0=5 * branch, g_col0=6 * branch)
    x2 = _outproj(a_out.reshape(t, branch), b_out.reshape(t, branch), x2,
                  ev_w_out[0].astype(BF16), tm=512)

    heads = ssd_d.shape[1]
    groups = heads // SSD_GROUP_HEADS
    n_main = 5 * branch + branch + 2 * groups * SSD_STATE
    w_in = od_w_in[0]
    w_dt = w_in[:, n_main:].reshape(d, 2, groups, SSD_GROUP_HEADS)
    w_dt = jnp.transpose(w_dt, (0, 2, 1, 3)).reshape(d, 2 * heads)
    w_dt = jnp.pad(w_dt, ((0, 0), (0, LANES - 2 * heads)))
    w_main = _cast_transposed(w_in.T, n_main, tn=1024)
    n_bc = 2 * groups * SSD_STATE
    n_wide = n_main - n_bc
    p = _inproj(x2, row(norm_w[1]), w_main, n=n_wide, tm=1024, tn=2048)
    bc, dt = _inproj(x2, row(norm_w[1]), w_main, w_dt.astype(BF16), col0=n_wide, n=n_bc, tm=1024, tn=n_bc)
    p3 = p.reshape(bsz, length, n_wide)
    c_out = _sconv(p3, sconv_w[0].astype(F32), tn=512)
    a_neg = -jnp.exp(ssd_a_log[0].astype(F32))
    d_out = _ssd(p3, bc.reshape(bsz, length, n_bc), dt.reshape(bsz, length, LANES),
                 ssd_conv_w[0].astype(F32), row(ssd_conv_b[0]),
                 _ssd_group_lanes(ssd_dt_bias[0].astype(F32), groups),
                 _ssd_group_lanes(a_neg, groups),
                 row(jnp.repeat(ssd_d[0].astype(F32), SSD_HEAD)),
                 x_col0=5 * branch, z_col0=4 * branch, chunk=SSD_CHUNK)
    out = _outproj(c_out.reshape(t, branch), d_out.reshape(t, branch), x2,
                   od_w_out[0].astype(BF16), row(ssd_norm_w[0]), row(final_norm_w), tm=512)
    return out.reshape(bsz, length, d)
```

```python
import functools

import jax
import jax.numpy as jnp
from jax import lax
from jax.experimental import pallas as pl
from jax.experimental.pallas import tpu as pltpu

F32 = jnp.float32
BF16 = jnp.bfloat16
EPS = 1e-6
LOG2E = 1.4426950408889634

VMEM_LIMIT_BYTES = 56 * 1024 * 1024
LANES = 128
BF16_SUBLANES = 16

HGRN_HEAD = 128
HGRN_CHUNK = 64
FNET_GROUP = 256
FNET_TABLE_ROWS = 64
OUTPROJ_ROW_PARTS = 2
FNET_GROUPS_PER_STEP = 1
FNET_K_CHUNK = 1024
SSD_HEAD = 64
SSD_STATE = 128
SSD_GROUP_HEADS = 8
SSD_GROUP_W = SSD_HEAD * SSD_GROUP_HEADS
SSD_CHUNK = 128


def _params(*sem):
    return pltpu.CompilerParams(dimension_semantics=sem, vmem_limit_bytes=VMEM_LIMIT_BYTES)


def _dot(a, b):
    return jnp.dot(a, b, preferred_element_type=F32)


def _dot_nt(a, b):
    return lax.dot_general(a, b, (((1,), (1,)), ((), ())), preferred_element_type=F32)


def _dot_tn(a, b):
    return lax.dot_general(a, b, (((0,), (0,)), ((), ())), preferred_element_type=F32)


def _split2(x):
    hi = x.astype(BF16)
    lo = (x - hi.astype(F32)).astype(BF16)
    return hi, lo


def _sel_dot_rhs(sel_sel, x):
    return _dot(sel_sel, jnp.concatenate(_split2(x), axis=0))


def _sel_dot_lhs(x, sel_sel):
    return _dot(jnp.concatenate(_split2(x), axis=1), sel_sel)


def _silu(x):
    return x * jax.nn.sigmoid(x)


def _rms_scale(x):
    return lax.rsqrt(jnp.mean(x * x, axis=-1, keepdims=True) + EPS)


def _cast_t_kernel(wt_ref, o_ref):
    o_ref[...] = wt_ref[...].T.astype(o_ref.dtype)


def _cast_transposed(wt, n, *, tn):
    d = wt.shape[1]
    return pl.pallas_call(
        _cast_t_kernel, grid=(n // tn,),
        in_specs=[pl.BlockSpec((tn, d), lambda j: (j, 0))],
        out_specs=pl.BlockSpec((d, tn), lambda j: (0, j)),
        out_shape=jax.ShapeDtypeStruct((d, n), BF16),
        compiler_params=_params("parallel"), name="cast_t",
    )(wt)


def _norm_rows_to(h_ref, x_ref, nw_ref, row0, n_rows, rows_per_step):
    nw = nw_ref[...]
    for r in range(row0, row0 + n_rows, rows_per_step):
        x = x_ref[r:r + rows_per_step, :]
        h_ref[r:r + rows_per_step, :] = (x * _rms_scale(x) * nw).astype(BF16)


def _inproj_kernel(x_ref, nw_ref, w_ref, o_ref, h_ref):
    tm = x_ref.shape[0]
    first = pl.program_id(1) == 0

    @pl.when(first)
    def _():
        half = tm // 2
        for row0 in (0, half):
            _norm_rows_to(h_ref, x_ref, nw_ref, row0, half, min(128, half))
            rows = slice(row0, row0 + half)
            o_ref[rows, :] = _dot(h_ref[rows, :], w_ref[...]).astype(o_ref.dtype)

    @pl.when(jnp.logical_not(first))
    def _():
        o_ref[...] = _dot(h_ref[...], w_ref[...]).astype(o_ref.dtype)


def _inproj(x2, nw, w, *, n, tm, tn, keep_h=False):
    t, d = x2.shape
    tm = min(tm, t)
    grid = (t // tm, n // tn)
    in_specs = [
        pl.BlockSpec((tm, d), lambda i, j: (i, 0)),
        pl.BlockSpec((1, d), lambda i, j: (0, 0)),
        pl.BlockSpec((d, tn), lambda i, j: (0, j)),
    ]
    out_main = pl.BlockSpec((tm, tn), lambda i, j: (i, j))
    if not keep_h:
        return pl.pallas_call(
            _inproj_kernel, grid=grid, in_specs=in_specs, out_specs=out_main,
            out_shape=jax.ShapeDtypeStruct((t, n), BF16), scratch_shapes=[pltpu.VMEM((tm, d), BF16)],
            compiler_params=_params("parallel", "arbitrary"), name="inproj",
        )(x2, nw, w)
    return pl.pallas_call(
        _inproj_kernel, grid=grid, in_specs=in_specs,
        out_specs=[out_main, pl.BlockSpec((tm, d), lambda i, j: (i, 0))],
        out_shape=[jax.ShapeDtypeStruct((t, n), BF16), jax.ShapeDtypeStruct((t, d), BF16)],
        compiler_params=_params("parallel", "arbitrary"), name="inproj_keep",
    )(x2, nw, w)


def _proj_tail_kernel(h_ref, w_ref, wdt_ref, o_ref, dt_ref):
    h = h_ref[...]
    o_ref[...] = _dot(h, w_ref[...]).astype(o_ref.dtype)
    dt_ref[...] = _dot(h, wdt_ref[...])


def _proj_tail(h, w, wdt, *, col0, n, tm):
    t, d = h.shape
    tm = min(tm, t)
    ndt = wdt.shape[1]
    rows = lambda width: pl.BlockSpec((tm, width), lambda i: (i, 0))
    return pl.pallas_call(
        _proj_tail_kernel, grid=(t // tm,),
        in_specs=[rows(d), pl.BlockSpec((d, n), lambda i: (0, col0 // n)), pl.BlockSpec((d, ndt), lambda i: (0, 0))],
        out_specs=[rows(n), rows(ndt)],
        out_shape=[jax.ShapeDtypeStruct((t, n), BF16), jax.ShapeDtypeStruct((t, ndt), F32)],
        compiler_params=_params("parallel"), name="proj_tail",
    )(h, w, wdt)


def _outproj_kernel(*refs, norm_b, final_norm):
    a_ref, b_ref, x_ref, w_ref = refs[:4]
    rest = list(refs[4:])
    o_ref = rest.pop()
    ka = a_ref.shape[1]
    bnw_ref = rest.pop(0) if norm_b else None
    fnw_ref = rest.pop(0) if final_norm else None
    part = o_ref.shape[0] // OUTPROJ_ROW_PARTS
    for rows in (slice(k * part, (k + 1) * part) for k in range(OUTPROJ_ROW_PARTS)):
        b = b_ref[rows, :]
        if norm_b:
            bf = b.astype(F32)
            b = (bf * _rms_scale(bf) * bnw_ref[...]).astype(BF16)
        acc = _dot(a_ref[rows, :], w_ref[0:ka, :]) + _dot(b, w_ref[ka:, :]) + x_ref[rows, :]
        if final_norm:
            acc = acc * _rms_scale(acc) * fnw_ref[...]
        o_ref[rows, :] = acc


def _outproj(a, b, x2, w, b_norm_w=None, final_norm_w=None, *, tm):
    t, d = x2.shape
    ka, kb = a.shape[1], b.shape[1]
    tm = min(tm, t)
    row = lambda width: pl.BlockSpec((tm, width), lambda i: (i, 0))
    const = lambda shape: pl.BlockSpec(shape, lambda i: (0, 0))
    in_specs = [row(ka), row(kb), row(d), const((ka + kb, d))]
    args = [a, b, x2, w]
    if b_norm_w is not None:
        in_specs.append(const((1, kb)))
        args.append(b_norm_w)
    if final_norm_w is not None:
        in_specs.append(const((1, d)))
        args.append(final_norm_w)
    kern = functools.partial(_outproj_kernel, norm_b=b_norm_w is not None,
                             final_norm=final_norm_w is not None)
    return pl.pallas_call(
        kern, grid=(t // tm,), in_specs=in_specs, out_specs=row(d),
        out_shape=jax.ShapeDtypeStruct((t, d), F32),
        compiler_params=_params("parallel"), name="outproj",
    )(*args)


def _hgrn_kernel(q_ref, v_ref, zf_ref, zb_ref, g_ref, lb_ref, nw_ref, o_ref,
                 qin_s, kin_s, qdec_s, kendt_s, v_s, dec_s, acc_s, *, chunk, block, n_heads):
    length, width = acc_s.shape
    n_chunks = length // chunk
    n_blocks = length // block
    per_block = block // chunk
    mid = chunk // 2
    z_refs = (zf_ref, zb_ref)
    edge = (chunk - 1, 0)

    def prepare(blk, st):
        lb = lb_ref[...]
        one_m_lb = 1.0 - lb
        bri = lax.broadcasted_iota(jnp.int32, (block, block), 0)
        bci = lax.broadcasted_iota(jnp.int32, (block, block), 1)
        same_chunk = (bri // chunk) == (bci // chunk)
        r0 = pl.multiple_of(blk * block, block)
        rows = pl.ds(r0, block)
        q = q_ref[rows, :].astype(F32)
        v_s[st, rows, :] = v_ref[rows, :]
        for d in (0, 1):
            tri = jnp.where(same_chunk & ((bri <= bci) if d else (bri >= bci)), 1.0, 0.0).astype(BF16)
            z = z_refs[d][rows, :].astype(F32)
            sg = jax.nn.sigmoid(z)
            logf = jnp.log2(lb + one_m_lb * sg)
            key = one_m_lb * (1.0 - sg)
            b2 = _dot(tri, jnp.concatenate(_split2(logf), axis=1))
            b = b2[:, :width] + b2[:, width:]
            for k in range(per_block):
                sl = slice(k * chunk, (k + 1) * chunk)
                rk = pl.ds(r0 + k * chunk, chunk)
                bk = b[sl]
                b_mid = bk[mid:mid + 1, :]
                b_edge = bk[edge[d]:edge[d] + 1, :]
                q_in = q[sl] * jnp.exp2(bk - b_mid)
                k_in = key[sl] * jnp.exp2(b_mid - bk)
                qin_s[st, d, rk, :] = q_in.astype(BF16)
                kin_s[st, d, rk, :] = k_in.astype(BF16)
                qdec_s[st, d, rk, :] = (q_in * jnp.exp2(b_mid)).astype(BF16)
                idx = d * n_chunks + blk * per_block + k
                kendt_s[st, idx] = (k_in * jnp.exp2(b_edge - b_mid)).T.astype(BF16)
                dec_s[st, idx] = jnp.broadcast_to(jnp.exp2(b_edge), (width, width)).T

    def scan(c, states, st, finalize):
        ri = lax.broadcasted_iota(jnp.int32, (chunk, chunk), 0)
        ci = lax.broadcasted_iota(jnp.int32, (chunk, chunk), 1)
        new_states = []
        for d in (0, 1):
            cc = c if d == 0 else n_chunks - 1 - c
            rows = pl.ds(pl.multiple_of(cc * chunk, chunk), chunk)
            idx = d * n_chunks + cc
            v = v_s[st, rows, :]
            s = _dot_nt(qin_s[st, d, rows, :], kin_s[st, d, rows, :])
            s = jnp.where((ri <= ci) if d else (ri >= ci), s, 0.0).astype(BF16)
            o = _dot(jnp.concatenate([qdec_s[st, d, rows, :], s], axis=1),
                     jnp.concatenate([states[d].astype(BF16), v], axis=0))
            new_states.append(states[d] * dec_s[st, idx] + _dot(kendt_s[st, idx], v))
            if finalize:
                tot = acc_s[rows, :] + o
                g = g_ref[rows, :].astype(F32)
                o_ref[rows, :] = (tot * _rms_scale(tot) * nw_ref[...] * _silu(g)).astype(o_ref.dtype)
            else:
                acc_s[rows, :] = o
        return tuple(new_states)

    half = n_chunks // 2
    unroll = min(8, half)
    trips = half // unroll
    blocks_per_trip = n_blocks // (2 * trips)

    def run(prep_set, scan_set):
        def trip(i, states, *, finalize):
            if scan_set is not None:
                for u in range(unroll):
                    states = scan(i * unroll + u, states, scan_set, finalize)
            if prep_set is not None:
                for u in range(blocks_per_trip):
                    prepare(i * blocks_per_trip + u, prep_set)
            return states

        zero = jnp.zeros((width, width), F32)
        states = lax.fori_loop(0, trips, functools.partial(trip, finalize=False), (zero, zero))
        lax.fori_loop(trips, 2 * trips, functools.partial(trip, finalize=True), states)

    step = pl.program_id(0)
    inner = (step > 0) & (step < n_heads)
    pl.when(step == 0)(lambda: run(0, None))
    pl.when(inner & (step % 2 == 1))(lambda: run(1, 0))
    pl.when(inner & (step % 2 == 0))(lambda: run(0, 1))
    pl.when(step == n_heads)(lambda: run(None, (n_heads - 1) % 2))


def _hgrn(p3, lb, nw, *, col0, chunk):
    bsz, length, _ = p3.shape
    width = HGRN_HEAD
    branch = lb.shape[1]
    heads = branch // width
    n_heads = bsz * heads
    c0 = col0 // width
    n_chunks = length // chunk
    cur = lambda s: jnp.minimum(s, n_heads - 1)
    prev = lambda s: jnp.maximum(s - 1, 0)
    seq = lambda k, at: pl.BlockSpec((None, length, width),
                                     lambda s: (at(s) // heads, 0, c0 + k * heads + at(s) % heads))
    vec = lambda at: pl.BlockSpec((1, width), lambda s: (0, at(s) % heads))
    scaled = pltpu.VMEM((2, 2, length, width), BF16)
    return pl.pallas_call(
        functools.partial(_hgrn_kernel, chunk=chunk, block=min(128, length), n_heads=n_heads),
        grid=(n_heads + 1,),
        in_specs=[seq(0, cur), seq(1, cur), seq(2, cur), seq(3, cur), seq(4, prev), vec(cur), vec(prev)],
        out_specs=pl.BlockSpec((None, length, width), lambda s: (prev(s) // heads, 0, prev(s) % heads)),
        out_shape=jax.ShapeDtypeStruct((bsz, length, branch), BF16),
        scratch_shapes=[scaled, scaled, scaled,
                        pltpu.VMEM((2, 2 * n_chunks, width, chunk), BF16),
                        pltpu.VMEM((2, length, width), BF16),
                        pltpu.VMEM((2, 2 * n_chunks, width, width), F32),
                        pltpu.VMEM((length, width), F32)],
        compiler_params=_params("arbitrary"), name="hgrn2",
    )(p3, p3, p3, p3, p3, lb, nw)


def _dft_cos_sin(n):
    k = lax.broadcasted_iota(jnp.int32, (n, n), 0) * lax.broadcasted_iota(jnp.int32, (n, n), 1)
    ang = (k % n).astype(F32) * (2.0 * jnp.pi / n)
    return jnp.cos(ang).astype(BF16), jnp.sin(ang).astype(BF16)


def _dft_rows(rows, n):
    k = rows[:, None] * lax.broadcasted_iota(jnp.int32, (rows.shape[0], n), 1)
    ang = (k % n).astype(F32) * (2.0 * jnp.pi / n)
    return jnp.cos(ang), jnp.sin(ang)


def _fnet_kernel(t1c_ref, t1s_ref, t2c_ref, t2s_ref, ccsc_ref, u_ref, fw_ref, fb_ref, g_ref, o_ref,
                 cs_s, *, scale):
    rb = t1c_ref.shape[0]
    length = u_ref.shape[0]
    gd = fw_ref.shape[1]

    @pl.when((pl.program_id(0) == 0) & (pl.program_id(1) == 0))
    def _():
        def gen(a, carry):
            rows = pl.ds(pl.multiple_of(a * rb, rb), rb)
            c2, s2 = t2c_ref[pl.ds(a, 1), :], t2s_ref[pl.ds(a, 1), :]
            c1, s1 = t1c_ref[...], t1s_ref[...]
            cs_s[rows, :length] = (c2 * c1 - s2 * s1).astype(BF16)
            cs_s[rows, length:] = (-(s2 * c1 + c2 * s1)).astype(BF16)
            return carry

        lax.fori_loop(0, t2c_ref.shape[0], gen, 0)

    for k in range(fw_ref.shape[0]):
        cols = slice(k * gd, (k + 1) * gd)
        z = _dot(u_ref[:, cols], ccsc_ref[...])
        zz = jnp.concatenate([z[:, :gd], z[:, gd:]], axis=0).astype(BF16)
        kc = min(FNET_K_CHUNK, 2 * length)
        mixed = _dot(cs_s[:, :kc], zz[:kc])
        for k0 in range(kc, 2 * length, kc):
            mixed = mixed + _dot(cs_s[:, k0:k0 + kc], zz[k0:k0 + kc])
        mixed = mixed * scale
        y = _dot(mixed.astype(BF16), fw_ref[k]) + fb_ref[:, cols]
        o_ref[:, cols] = (y * _silu(g_ref[:, cols].astype(F32))).astype(o_ref.dtype)


def _fnet(p3, fw, fb, *, u_col0, g_col0):
    bsz, length, _ = p3.shape
    groups, gd, _ = fw.shape
    branch = groups * gd
    ccsc = jnp.concatenate(_dft_cos_sin(gd), axis=1)
    rb = FNET_TABLE_ROWS
    t1c, t1s = _dft_rows(jnp.arange(rb, dtype=jnp.int32), length)
    t2c, t2s = _dft_rows(jnp.arange(length // rb, dtype=jnp.int32) * rb, length)
    gps = FNET_GROUPS_PER_STEP
    wide = gps * gd
    seq = lambda c0: pl.BlockSpec((None, length, wide), lambda b, g: (b, 0, c0 + g))
    table = lambda rows: pl.BlockSpec((rows, length), lambda b, g: (0, 0))
    return pl.pallas_call(
        functools.partial(_fnet_kernel, scale=float((length * gd) ** -0.5)),
        grid=(bsz, groups // gps),
        in_specs=[table(rb), table(rb), table(length // rb), table(length // rb),
                  pl.BlockSpec((gd, 2 * gd), lambda b, g: (0, 0)),
                  seq(u_col0 // wide),
                  pl.BlockSpec((gps, gd, gd), lambda b, g: (g, 0, 0)),
                  pl.BlockSpec((1, wide), lambda b, g: (0, g)),
                  seq(g_col0 // wide)],
        out_specs=seq(0),
        out_shape=jax.ShapeDtypeStruct((bsz, length, branch), BF16),
        scratch_shapes=[pltpu.VMEM((length, 2 * length), BF16)],
        compiler_params=_params("arbitrary", "arbitrary"), name="fnet",
    )(t1c, t1s, t2c, t2s, ccsc, p3, fw, fb, p3)


class _RowShift:
    def __init__(self, chunk, length):
        self.chunk, self.length, self.halo = chunk, length, BF16_SUBLANES
        self.ext = chunk + 2 * self.halo
        si = lax.broadcasted_iota(jnp.int32, (chunk, self.ext), 0)
        sj = lax.broadcasted_iota(jnp.int32, (chunk, self.ext), 1)
        self.mats = [jnp.concatenate([jnp.where(sj == si + (off - 1), 1.0, 0.0),
                                      jnp.where(sj == si + (off + 1), 1.0, 0.0)], axis=0).astype(BF16)
                     for off in (0, self.halo, 2 * self.halo)]

    def window(self, r0):
        start = pl.multiple_of(jnp.clip(r0 - self.halo, 0, self.length - self.ext), self.halo)
        off = r0 - start
        mat = jnp.where(off == self.halo, self.mats[1], jnp.where(off == 0, self.mats[0], self.mats[2]))
        return pl.ds(start, self.ext), mat


def _conv3_rows(load_rows, r0, n_rows, length, w):
    h = BF16_SUBLANES
    u = load_rows(r0, n_rows)
    prev_blk = load_rows(pl.multiple_of(jnp.maximum(r0 - h, 0), h), h)
    next_blk = load_rows(pl.multiple_of(jnp.minimum(r0 + n_rows, length - h), h), h)
    prev_row = jnp.where(r0 > 0, prev_blk[h - 1:h, :], 0.0)
    next_row = jnp.where(r0 + n_rows < length, next_blk[0:1, :], 0.0)
    ri = lax.broadcasted_iota(jnp.int32, u.shape, 0)
    up = jnp.where(ri == 0, prev_row, pltpu.roll(u, 1, 0))
    un = jnp.where(ri == n_rows - 1, next_row, pltpu.roll(u, n_rows - 1, 0))
    return up * w[0:1, :] + u * w[1:2, :] + un * w[2:3, :]


def _sconv_kernel(cin_ref, cb_ref, cc_ref, cg_ref, w_ref, o_ref, *, chunk):
    length, width = o_ref.shape

    def body(r, carry):
        r0 = pl.multiple_of(r * chunk, chunk)
        rows = pl.ds(r0, chunk)
        for j in range(width // LANES):
            cols = slice(j * LANES, (j + 1) * LANES)

            def load_u(start, size, cols=cols):
                rs = pl.ds(start, size)
                return cc_ref[rs, cols].astype(F32) * cin_ref[rs, cols].astype(F32)

            conv = _conv3_rows(load_u, r0, chunk, length, w_ref[:, cols])
            o_ref[rows, cols] = (cb_ref[rows, cols].astype(F32) * conv
                                 * _silu(cg_ref[rows, cols].astype(F32))).astype(o_ref.dtype)
        return carry

    lax.fori_loop(0, length // chunk, body, 0)


def _sconv(p3, w, *, tn):
    bsz, length, _ = p3.shape
    branch = w.shape[1]
    nb = branch // tn
    seq = lambda k: pl.BlockSpec((None, length, tn), lambda b, j, k=k: (b, 0, k * nb + j))
    return pl.pallas_call(
        functools.partial(_sconv_kernel, chunk=min(128, length)),
        grid=(bsz, nb),
        in_specs=[seq(0), seq(1), seq(2), seq(3), pl.BlockSpec((3, tn), lambda b, j: (0, j))],
        out_specs=seq(0),
        out_shape=jax.ShapeDtypeStruct((bsz, length, branch), BF16),
        compiler_params=_params("parallel", "parallel"), name="sconv",
    )(p3, p3, p3, p3, w)


def _ssd_kernel(x_ref, bm_ref, cm_ref, z_ref, dt_ref,
                wx_ref, wb_ref, wc_ref, bx_ref, bb_ref, bc_ref,
                dtb_ref, a_ref, dsk_ref,
                o_ref,
                bmt_s, cm_s, yacc_s, ecol_s, xdte_s, alast_s, st_s,
                *staging, chunk):
    length, gw = yacc_s.shape
    n_chunks = length // chunk
    heads = SSD_GROUP_HEADS
    hd = gw // heads
    xs_t, ae_t, de_t, ac_t, at_t, cb_t = zip(staging[:6], staging[6:])
    shifter = _RowShift(chunk, length)
    edge = (chunk - 1, 0)

    ri = lax.broadcasted_iota(jnp.int32, (chunk, chunk), 0)
    ci = lax.broadcasted_iota(jnp.int32, (chunk, chunk), 1)
    masks = (ri >= ci, ri <= ci)
    tris = []
    for m in masks:
        tri = jnp.where(m, 1.0, 0.0).astype(BF16)
        tris.append(jnp.concatenate([tri, tri], axis=1))
    low = lax.broadcasted_iota(jnp.int32, (chunk, LANES), 1).astype(F32).astype(BF16) < hd
    sel_l =lax.broadcasted_iota(jnp.int32, (LANES, gw), 0)
    sel_c = lax.broadcasted_iota(jnp.int32, (LANES, gw), 1) // hd
    expands = []
    for d in (0, 1):
        e = jnp.where(sel_l == sel_c + d * heads, 1.0, 0.0).astype(BF16)
        expands.append(jnp.concatenate([e, e], axis=0))
    a2_row = a_ref[...] * LOG2E

    n_pairs = gw // LANES

    def front_steps(c, slot):
        env = {}

        def conv_silu(ref, w_ref, b_ref, cols):
            both = _dot(env["shift"], ref[env["erows"], cols])
            w = w_ref[:, cols]
            y = (both[:chunk] * w[0:1, :] + ref[env["rows"], cols].astype(F32) * w[1:2, :]
                 + both[chunk:] * w[2:3, :] + b_ref[:, cols])
            return _silu(y)

        def b_conv():
            r0 = pl.multiple_of(c * chunk, chunk)
            env["rows"] = pl.ds(r0, chunk)
            env["erows"], env["shift"] = shifter.window(r0)
            env["bmt"] = bmt_c = conv_silu(bm_ref, wb_ref, bb_ref, slice(0, LANES)).T.astype(BF16)
            bmt_s[c] = bmt_c

        def c_conv():
            cm_c = conv_silu(cm_ref, wc_ref, bc_ref, slice(0, LANES)).astype(BF16)
            cm_s[env["rows"], :] = cm_c
            cb_t[slot][...] = _dot(cm_c, env["bmt"])

        def decay(d):
            if d == 0:
                to_front = (LANES - pl.program_id(1) * 2 * heads) % LANES
                t = pltpu.roll(dt_ref[env["rows"], :], to_front, 1) + dtb_ref[...]
                env["dtc"] = jnp.maximum(t, 0.0) + jnp.log1p(jnp.exp(-jnp.abs(t)))
            a = _sel_dot_rhs(tris[d], env["dtc"] * a2_row)
            ac_t[slot][d] = a
            at_t[slot][d] = a.T

        def spread(d):
            ae_t[slot][d] = _dot(jnp.concatenate(_split2(ac_t[slot][d]), axis=1), expands[d])
            de_t[slot][d] = _dot(jnp.concatenate(_split2(env["dtc"]), axis=1), expands[d])

        def x_conv(s):
            wide = gw // 2
            wcols = slice(s * wide, (s + 1) * wide)
            xs_t[slot][:, wcols] = conv_silu(x_ref, wx_ref, bx_ref, wcols)

        part = functools.partial
        return [b_conv, c_conv, part(decay, 0), part(decay, 1), part(spread, 0), part(spread, 1),
                part(x_conv, 0), part(x_conv, 1)]

    def back_unit(c, slot, p, d, acc):
        rows = pl.ds(pl.multiple_of(c * chunk, chunk), chunk)
        slab = slice(p * LANES, (p + 1) * LANES)
        xs_p = xs_t[slot][:, slab]
        if d == 0:
            acc = xs_p * dsk_ref[:, slab]
        l0 = d * heads + 2 * p
        acum_p = ae_t[slot][d, :, slab]
        xdt = xs_p * de_t[slot][d, :, slab]
        a_edge = acum_p[edge[d]:edge[d] + 1, :]
        ecol_s[d, rows, slab] = jnp.exp2(acum_p).astype(BF16)
        xdte_s[d, rows, slab] = (xdt * jnp.exp2(a_edge - acum_p)).astype(BF16)
        alast_s[d * n_chunks + c, :, slab] = jnp.broadcast_to(jnp.exp2(a_edge), (8, LANES))
        ms = []
        for l in (l0, l0 + 1):
            diff = ac_t[slot][d, :, l:l + 1] - at_t[slot][d, l:l + 1, :]
            ms.append((cb_t[slot][...] * jnp.exp2(jnp.where(masks[d], diff, -1e30))).astype(BF16))
        xb = xdt.astype(BF16)
        zero = jnp.zeros_like(xb)
        block_diag = jnp.concatenate([jnp.where(low, xb, zero), jnp.where(low, zero, xb)], axis=0)
        acc = acc + _dot(jnp.concatenate(ms, axis=1), block_diag)
        if d == 1:
            yacc_s[rows, slab] = acc
        return acc

    def prepare_two(i, carry):
        c0 = 2 * i
        for c_back, s_back, c_front, s_front in ((c0, 0, c0 + 1, 1),
                                                 (c0 + 1, 1, jnp.minimum(c0 + 2, n_chunks - 1), 0)):
            steps = front_steps(c_front, s_front)
            acc = None
            for p in range(n_pairs):
                for d in (0, 1):
                    acc = back_unit(c_back, s_back, p, d, acc)
                    steps[2 * p + d]()
        return carry

    for step in front_steps(0, 0):
        step()
    lax.fori_loop(0, n_chunks // 2, prepare_two, 0)

    st_s[...] = jnp.zeros_like(st_s)

    def scan(c, carry, *, finalize):
        for d in (0, 1):
            cc = c if d == 0 else n_chunks - 1 - c
            rows = pl.ds(pl.multiple_of(cc * chunk, chunk), chunk)
            state = st_s[d]
            y = _dot(cm_s[rows, :], state.astype(BF16)) * ecol_s[d, rows, :].astype(F32)
            st_s[d] = state * alast_s[d * n_chunks + cc][0:1, :] + _dot(bmt_s[cc], xdte_s[d, rows, :])
            if finalize:
                tot = yacc_s[rows, :] + y
                o_ref[rows, :] = (tot * _silu(z_ref[rows, :].astype(F32))).astype(o_ref.dtype)
            else:
                yacc_s[rows, :] = yacc_s[rows, :] + y
        return carry

    half = n_chunks // 2
    lax.fori_loop(0, half, functools.partial(scan, finalize=False), 0)
    lax.fori_loop(half, n_chunks, functools.partial(scan, finalize=True), 0)


def _ssd(p3, bc3, dt3, conv_w, conv_b, dt_bias_g, a_g, dsk, *, x_col0, z_col0, chunk):
    bsz, length, _ = p3.shape
    gw, ns = SSD_GROUP_W, SSD_STATE
    branch = dsk.shape[1]
    groups = branch // gw
    xb, bb, cb = x_col0 // gw, 0, groups
    seq =lambda w, c0: pl.BlockSpec((None, length, w), lambda b, g: (b, 0, c0 + g))
    par = lambda rows, w, c0: pl.BlockSpec((rows, w), lambda b, g: (0, c0 + g))
    n_chunks = length // chunk
    return pl.pallas_call(
        functools.partial(_ssd_kernel, chunk=chunk),
        grid=(bsz, groups),
        in_specs=[seq(gw, xb), seq(ns, bb), seq(ns, cb), seq(gw, z_col0 // gw),
                  pl.BlockSpec((None, length, LANES), lambda b, g: (b, 0, 0)),
                  par(3, gw, 0), par(3, ns, branch // ns), par(3, ns, (branch + groups * ns) // ns),
                  par(1, gw, 0), par(1, ns, branch // ns), par(1, ns, (branch + groups * ns) // ns),
                  par(1, LANES, 0), par(1, LANES, 0),
                  par(1, gw, 0)],
        out_specs=seq(gw, 0),
        out_shape=jax.ShapeDtypeStruct((bsz, length, branch), BF16),
        scratch_shapes=[pltpu.VMEM((n_chunks, ns, chunk), BF16),
                        pltpu.VMEM((length, ns), BF16),
                        pltpu.VMEM((length, gw), F32),
                        pltpu.VMEM((2, length, gw), BF16),
                        pltpu.VMEM((2, length, gw), BF16),
                        pltpu.VMEM((2 * n_chunks, 8, gw), F32),
                        pltpu.VMEM((2, ns, gw), F32),
                        ] + 2 * [
                        pltpu.VMEM((chunk, gw), F32),
                        pltpu.VMEM((2, chunk, gw), F32),
                        pltpu.VMEM((2, chunk, gw), F32),
                        pltpu.VMEM((2, chunk, LANES), F32),
                        pltpu.VMEM((2, LANES, chunk), F32),
                        pltpu.VMEM((chunk, chunk), F32)],
        compiler_params=_params("parallel", "parallel"), name="ssd",
    )(p3, bc3, bc3, p3, dt3, conv_w, conv_w, conv_w, conv_b, conv_b, conv_b,
      dt_bias_g, a_g, dsk)


def _ssd_group_lanes(v, groups):
    hpg = v.shape[1] // groups
    per = jnp.concatenate([v[0].reshape(groups, hpg), v[1].reshape(groups, hpg)], axis=1)
    per = jnp.pad(per, ((0, 0), (0, LANES - 2 * hpg)))
    return per.reshape(1, groups * LANES)


def kernel(x, norm_w, final_norm_w, ev_w_in, ev_w_out, hgrn_lb_logits, hgrn_norm_w, fnet_w, fnet_b,
           od_w_in, od_w_out, sconv_w, ssd_conv_w, ssd_conv_b, ssd_dt_bias, ssd_a_log, ssd_d, ssd_norm_w):
    bsz, length, d = x.shape
    t = bsz * length
    branch = d
    x2 = x.reshape(t, d)
    row = lambda v: v.astype(F32).reshape(1, -1)

    lower_bounds = jnp.cumsum(jax.nn.softmax(hgrn_lb_logits.astype(F32), axis=0), axis=0)
    p = _inproj(x2, row(norm_w[0]), ev_w_in[0].astype(BF16), n=ev_w_in.shape[2], tm=1024, tn=2048)
    p3 = p.reshape(bsz, length, -1)
    a_out = _hgrn(p3, row(lower_bounds[0]), row(hgrn_norm_w[0]), col0=0, chunk=HGRN_CHUNK)
    b_out = _fnet(p3, fnet_w[0].astype(BF16), row(fnet_b[0]), u_col0=5 * branch, g_col0=6 * branch)
    x2 = _outproj(a_out.reshape(t, branch), b_out.reshape(t, branch), x2,
                  ev_w_out[0].astype(BF16), tm=512)

    heads = ssd_d.shape[1]
    groups = heads // SSD_GROUP_HEADS
    n_main = 5 * branch + branch + 2 * groups * SSD_STATE
    w_in = od_w_in[0]
    w_dt = w_in[:, n_main:].reshape(d, 2, groups, SSD_GROUP_HEADS)
    w_dt = jnp.transpose(w_dt, (0, 2, 1, 3)).reshape(d, 2 * heads)
    w_dt = jnp.pad(w_dt, ((0, 0), (0, LANES - 2 * heads)))
    w_main = _cast_transposed(w_in.T, n_main, tn=1024)
    n_bc = 2 * groups * SSD_STATE
    n_wide = n_main - n_bc
    p, h = _inproj(x2, row(norm_w[1]), w_main, n=n_wide, tm=1024, tn=2048, keep_h=True)
    bc, dt = _proj_tail(h, w_main, w_dt.astype(BF16), col0=n_wide, n=n_bc, tm=1024)
    p3 = p.reshape(bsz, length, n_wide)
    c_out = _sconv(p3, sconv_w[0].astype(F32), tn=512)
    a_neg = -jnp.exp(ssd_a_log[0].astype(F32))
    d_out = _ssd(p3, bc.reshape(bsz, length, n_bc), dt.reshape(bsz, length, LANES),
                 ssd_conv_w[0].astype(F32), row(ssd_conv_b[0]),
                 _ssd_group_lanes(ssd_dt_bias[0].astype(F32), groups),
                 _ssd_group_lanes(a_neg, groups),
                 row(jnp.repeat(ssd_d[0].astype(F32), SSD_HEAD)),
                 x_col0=5 * branch, z_col0=4 * branch, chunk=SSD_CHUNK)
    out = _outproj(c_out.reshape(t, branch), d_out.reshape(t, branch), x2,
                   od_w_out[0].astype(BF16), row(ssd_norm_w[0]), row(final_norm_w), tm=512)
    return out.reshape(bsz, length, d)
```

```python
import functools

import jax
import jax.numpy as jnp
from jax import lax
from jax.experimental import pallas as pl
from jax.experimental.pallas import tpu as pltpu

F32 = jnp.float32
BF16 = jnp.bfloat16
EPS = 1e-6
LOG2E = 1.4426950408889634

VMEM_LIMIT_BYTES = 56 * 1024 * 1024
LANES = 128
BF16_SUBLANES = 16

HGRN_HEAD = 128
HGRN_CHUNK = 64
FNET_GROUP = 256
FNET_TABLE_ROWS = 64
FNET_GROUPS_PER_STEP = 1
FNET_K_CHUNK = 1024
SSD_HEAD = 64
SSD_STATE = 128
SSD_GROUP_HEADS = 8
SSD_GROUP_W = SSD_HEAD * SSD_GROUP_HEADS
SSD_CHUNK = 128
SSD_SCAN_UNROLL = 8
SSD_PREPARE_UNROLL = 4


def _params(*sem):
    return pltpu.CompilerParams(dimension_semantics=sem, vmem_limit_bytes=VMEM_LIMIT_BYTES)


def _dot(a, b):
    return jnp.dot(a, b, preferred_element_type=F32)


def _dot_nt(a, b):
    return lax.dot_general(a, b, (((1,), (1,)), ((), ())), preferred_element_type=F32)


def _dot_tn(a, b):
    return lax.dot_general(a, b, (((0,), (0,)), ((), ())), preferred_element_type=F32)


def _split2(x):
    hi = x.astype(BF16)
    lo = (x - hi.astype(F32)).astype(BF16)
    return hi, lo


def _sel_dot_rhs(sel_sel, x):
    return _dot(sel_sel, jnp.concatenate(_split2(x), axis=0))


def _sel_dot_lhs(x, sel_sel):
    return _dot(jnp.concatenate(_split2(x), axis=1), sel_sel)


def _silu(x):
    return x * jax.nn.sigmoid(x)


def _rms_scale(x):
    return lax.rsqrt(jnp.mean(x * x, axis=-1, keepdims=True) + EPS)


def _cast_t_kernel(wt_ref, o_ref):
    o_ref[...] = wt_ref[...].T.astype(o_ref.dtype)


def _cast_transposed(wt, n, *, tn):
    d = wt.shape[1]
    return pl.pallas_call(
        _cast_t_kernel, grid=(n // tn,),
        in_specs=[pl.BlockSpec((tn, d), lambda j: (j, 0))],
        out_specs=pl.BlockSpec((d, tn), lambda j: (0, j)),
        out_shape=jax.ShapeDtypeStruct((d, n), BF16),
        compiler_params=_params("parallel"), name="cast_t",
    )(wt)


def _norm_rows_to(h_ref, x_ref, nw_ref, row0, n_rows, rows_per_step):
    nw = nw_ref[...]
    for r in range(row0, row0 + n_rows, rows_per_step):
        x = x_ref[r:r + rows_per_step, :]
        h_ref[r:r + rows_per_step, :] = (x * _rms_scale(x) * nw).astype(BF16)


def _inproj_kernel(x_ref, nw_ref, w_ref, o_ref, h_ref):
    tm = x_ref.shape[0]
    first = pl.program_id(1) == 0

    @pl.when(first)
    def _():
        half = tm // 2
        for row0 in (0, half):
            _norm_rows_to(h_ref, x_ref, nw_ref, row0, half, min(128, half))
            rows = slice(row0, row0 + half)
            o_ref[rows, :] = _dot(h_ref[rows, :], w_ref[...]).astype(o_ref.dtype)

    @pl.when(jnp.logical_not(first))
    def _():
        o_ref[...] = _dot(h_ref[...], w_ref[...]).astype(o_ref.dtype)


def _inproj(x2, nw, w, *, n, tm, tn, keep_h=False):
    t, d = x2.shape
    tm = min(tm, t)
    grid = (t // tm, n // tn)
    in_specs = [
        pl.BlockSpec((tm, d), lambda i, j: (i, 0)),
        pl.BlockSpec((1, d), lambda i, j: (0, 0)),
        pl.BlockSpec((d, tn), lambda i, j: (0, j)),
    ]
    out_main = pl.BlockSpec((tm, tn), lambda i, j: (i, j))
    if not keep_h:
        return pl.pallas_call(
            _inproj_kernel, grid=grid, in_specs=in_specs, out_specs=out_main,
            out_shape=jax.ShapeDtypeStruct((t, n), BF16), scratch_shapes=[pltpu.VMEM((tm, d), BF16)],
            compiler_params=_params("parallel", "arbitrary"), name="inproj",
        )(x2, nw, w)
    return pl.pallas_call(
        _inproj_kernel, grid=grid, in_specs=in_specs,
        out_specs=[out_main, pl.BlockSpec((tm, d), lambda i, j: (i, 0))],
        out_shape=[jax.ShapeDtypeStruct((t, n), BF16), jax.ShapeDtypeStruct((t, d), BF16)],
        compiler_params=_params("parallel", "arbitrary"), name="inproj_keep",
    )(x2, nw, w)


def _proj_tail_kernel(h_ref, w_ref, wdt_ref, o_ref, dt_ref):
    h = h_ref[...]
    o_ref[...] = _dot(h, w_ref[...]).astype(o_ref.dtype)
    dt_ref[...] = _dot(h, wdt_ref[...])


def _proj_tail(h, w, wdt, *, col0, n, tm):
    t, d = h.shape
    tm = min(tm, t)
    ndt = wdt.shape[1]
    rows = lambda width: pl.BlockSpec((tm, width), lambda i: (i, 0))
    return pl.pallas_call(
        _proj_tail_kernel, grid=(t // tm,),
        in_specs=[rows(d), pl.BlockSpec((d, n), lambda i: (0, col0 // n)), pl.BlockSpec((d, ndt), lambda i: (0, 0))],
        out_specs=[rows(n), rows(ndt)],
        out_shape=[jax.ShapeDtypeStruct((t, n), BF16), jax.ShapeDtypeStruct((t, ndt), F32)],
        compiler_params=_params("parallel"), name="proj_tail",
    )(h, w, wdt)


def _outproj_kernel(*refs, norm_b, final_norm):
    a_ref, b_ref, x_ref, w_ref = refs[:4]
    rest = list(refs[4:])
    o_ref = rest.pop()
    ka = a_ref.shape[1]
    bnw_ref = rest.pop(0) if norm_b else None
    fnw_ref = rest.pop(0) if final_norm else None
    b = b_ref[...]
    if norm_b:
        bf = b.astype(F32)
        b = (bf * _rms_scale(bf) * bnw_ref[...]).astype(BF16)
    acc = _dot(a_ref[...], w_ref[0:ka, :]) + _dot(b, w_ref[ka:, :]) + x_ref[...]
    if final_norm:
        acc = acc * _rms_scale(acc) * fnw_ref[...]
    o_ref[...] = acc


def _outproj(a, b, x2, w, b_norm_w=None, final_norm_w=None, *, tm):
    t, d = x2.shape
    ka, kb = a.shape[1], b.shape[1]
    tm = min(tm, t)
    row = lambda width: pl.BlockSpec((tm, width), lambda i: (i, 0))
    const = lambda shape: pl.BlockSpec(shape, lambda i: (0, 0))
    in_specs = [row(ka), row(kb), row(d), const((ka + kb, d))]
    args = [a, b, x2, w]
    if b_norm_w is not None:
        in_specs.append(const((1, kb)))
        args.append(b_norm_w)
    if final_norm_w is not None:
        in_specs.append(const((1, d)))
        args.append(final_norm_w)
    kern = functools.partial(_outproj_kernel, norm_b=b_norm_w is not None,
                             final_norm=final_norm_w is not None)
    return pl.pallas_call(
        kern, grid=(t // tm,), in_specs=in_specs, out_specs=row(d),
        out_shape=jax.ShapeDtypeStruct((t, d), F32),
        compiler_params=_params("parallel"), name="outproj",
    )(*args)


def _hgrn_kernel(q_ref, v_ref, zf_ref, zb_ref, g_ref, lb_ref, nw_ref, o_ref,
                 qin_s, kin_s, qdec_s, kendt_s, v_s, dec_s, acc_s, *, chunk, block, n_heads):
    length, width = acc_s.shape
    n_chunks = length // chunk
    n_blocks = length // block
    per_block = block // chunk
    mid = chunk // 2
    z_refs = (zf_ref, zb_ref)
    edge = (chunk - 1, 0)

    def prepare(blk, st):
        lb = lb_ref[...]
        one_m_lb = 1.0 - lb
        bri = lax.broadcasted_iota(jnp.int32, (block, block), 0)
        bci = lax.broadcasted_iota(jnp.int32, (block, block), 1)
        same_chunk = (bri // chunk) == (bci // chunk)
        r0 = pl.multiple_of(blk * block, block)
        rows = pl.ds(r0, block)
        q = q_ref[rows, :].astype(F32)
        v_s[st, rows, :] = v_ref[rows, :]
        for d in (0, 1):
            tri = jnp.where(same_chunk & ((bri <= bci) if d else (bri >= bci)), 1.0, 0.0).astype(BF16)
            z = z_refs[d][rows, :].astype(F32)
            sg = jax.nn.sigmoid(z)
            logf = jnp.log2(lb + one_m_lb * sg)
            key = one_m_lb * (1.0 - sg)
            b2 = _dot(tri, jnp.concatenate(_split2(logf), axis=1))
            b = b2[:, :width] + b2[:, width:]
            for k in range(per_block):
                sl = slice(k * chunk, (k + 1) * chunk)
                rk = pl.ds(r0 + k * chunk, chunk)
                bk = b[sl]
                b_mid = bk[mid:mid + 1, :]
                b_edge = bk[edge[d]:edge[d] + 1, :]
                q_in = q[sl] * jnp.exp2(bk - b_mid)
                k_in = key[sl] * jnp.exp2(b_mid - bk)
                qin_s[st, d, rk, :] = q_in.astype(BF16)
                kin_s[st, d, rk, :] = k_in.astype(BF16)
                qdec_s[st, d, rk, :] = (q_in * jnp.exp2(b_mid)).astype(BF16)
                idx = d * n_chunks + blk * per_block + k
                kendt_s[st, idx] = (k_in * jnp.exp2(b_edge - b_mid)).T.astype(BF16)
                dec_s[st, idx] = jnp.broadcast_to(jnp.exp2(b_edge), (width, width)).T

    def scan(c, states, st, finalize):
        ri = lax.broadcasted_iota(jnp.int32, (chunk, chunk), 0)
        ci = lax.broadcasted_iota(jnp.int32, (chunk, chunk), 1)
        new_states = []
        for d in (0, 1):
            cc = c if d == 0 else n_chunks - 1 - c
            rows = pl.ds(pl.multiple_of(cc * chunk, chunk), chunk)
            idx = d * n_chunks + cc
            v = v_s[st, rows, :]
            s = _dot_nt(qin_s[st, d, rows, :], kin_s[st, d, rows, :])
            s = jnp.where((ri <= ci) if d else (ri >= ci), s, 0.0).astype(BF16)
            o = _dot(jnp.concatenate([qdec_s[st, d, rows, :], s], axis=1),
                     jnp.concatenate([states[d].astype(BF16), v], axis=0))
            new_states.append(states[d] * dec_s[st, idx] + _dot(kendt_s[st, idx], v))
            if finalize:
                tot = acc_s[rows, :] + o
                g = g_ref[rows, :].astype(F32)
                o_ref[rows, :] = (tot * _rms_scale(tot) * nw_ref[...] * _silu(g)).astype(o_ref.dtype)
            else:
                acc_s[rows, :] = o
        return tuple(new_states)

    half = n_chunks // 2
    unroll = min(16, half)
    trips = half // unroll
    blocks_per_trip = n_blocks // (2 * trips)

    def run(prep_set, scan_set):
        def trip(i, states, *, finalize):
            if scan_set is not None:
                for u in range(unroll):
                    states = scan(i * unroll + u, states, scan_set, finalize)
            if prep_set is not None:
                for u in range(blocks_per_trip):
                    prepare(i * blocks_per_trip + u, prep_set)
            return states

        zero = jnp.zeros((width, width), F32)
        states = lax.fori_loop(0, trips, functools.partial(trip, finalize=False), (zero, zero))
        lax.fori_loop(trips, 2 * trips, functools.partial(trip, finalize=True), states)

    step = pl.program_id(0)
    inner = (step > 0) & (step < n_heads)
    pl.when(step == 0)(lambda: run(0, None))
    pl.when(inner & (step % 2 == 1))(lambda: run(1, 0))
    pl.when(inner & (step % 2 == 0))(lambda: run(0, 1))
    pl.when(step == n_heads)(lambda: run(None, (n_heads - 1) % 2))


def _hgrn(p3, lb, nw, *, col0, chunk):
    bsz, length, _ = p3.shape
    width = HGRN_HEAD
    branch = lb.shape[1]
    heads = branch // width
    n_heads = bsz * heads
    c0 = col0 // width
    n_chunks = length // chunk
    cur = lambda s: jnp.minimum(s, n_heads - 1)
    prev = lambda s: jnp.maximum(s - 1, 0)
    seq = lambda k, at: pl.BlockSpec((None, length, width),
                                     lambda s: (at(s) // heads, 0, c0 + k * heads + at(s) % heads))
    vec = lambda at: pl.BlockSpec((1, width), lambda s: (0, at(s) % heads))
    scaled = pltpu.VMEM((2, 2, length, width), BF16)
    return pl.pallas_call(
        functools.partial(_hgrn_kernel, chunk=chunk, block=min(128, length), n_heads=n_heads),
        grid=(n_heads + 1,),
        in_specs=[seq(0, cur), seq(1, cur), seq(2, cur), seq(3, cur), seq(4, prev), vec(cur), vec(prev)],
        out_specs=pl.BlockSpec((None, length, width), lambda s: (prev(s) // heads, 0, prev(s) % heads)),
        out_shape=jax.ShapeDtypeStruct((bsz, length, branch), BF16),
        scratch_shapes=[scaled, scaled, scaled,
                        pltpu.VMEM((2, 2 * n_chunks, width, chunk), BF16),
                        pltpu.VMEM((2, length, width), BF16),
                        pltpu.VMEM((2, 2 * n_chunks, width, width), F32),
                        pltpu.VMEM((length, width), F32)],
        compiler_params=_params("arbitrary"), name="hgrn2",
    )(p3, p3, p3, p3, p3, lb, nw)


def _dft_cos_sin(n):
    k = lax.broadcasted_iota(jnp.int32, (n, n), 0) * lax.broadcasted_iota(jnp.int32, (n, n), 1)
    ang = (k % n).astype(F32) * (2.0 * jnp.pi / n)
    return jnp.cos(ang).astype(BF16), jnp.sin(ang).astype(BF16)


def _dft_rows(rows, n):
    k = rows[:, None] * lax.broadcasted_iota(jnp.int32, (rows.shape[0], n), 1)
    ang = (k % n).astype(F32) * (2.0 * jnp.pi / n)
    return jnp.cos(ang), jnp.sin(ang)


def _fnet_kernel(t1c_ref, t1s_ref, t2c_ref, t2s_ref, ccsc_ref, u_ref, fw_ref, fb_ref, g_ref, o_ref,
                 cs_s, *, scale):
    rb = t1c_ref.shape[0]
    length = u_ref.shape[0]
    gd = fw_ref.shape[1]

    @pl.when((pl.program_id(0) == 0) & (pl.program_id(1) == 0))
    def _():
        def gen(a, carry):
            rows = pl.ds(pl.multiple_of(a * rb, rb), rb)
            c2, s2 = t2c_ref[pl.ds(a, 1), :], t2s_ref[pl.ds(a, 1), :]
            c1, s1 = t1c_ref[...], t1s_ref[...]
            cs_s[rows, :length] = (c2 * c1 - s2 * s1).astype(BF16)
            cs_s[rows, length:] = (-(s2 * c1 + c2 * s1)).astype(BF16)
            return carry

        lax.fori_loop(0, t2c_ref.shape[0], gen, 0)

    for k in range(fw_ref.shape[0]):
        cols = slice(k * gd, (k + 1) * gd)
        z = _dot(u_ref[:, cols], ccsc_ref[...])
        zz = jnp.concatenate([z[:, :gd], z[:, gd:]], axis=0).astype(BF16)
        kc = min(FNET_K_CHUNK, 2 * length)
        mixed = _dot(cs_s[:, :kc], zz[:kc])
        for k0 in range(kc, 2 * length, kc):
            mixed = mixed + _dot(cs_s[:, k0:k0 + kc], zz[k0:k0 + kc])
        mixed = mixed * scale
        y = _dot(mixed.astype(BF16), fw_ref[k]) + fb_ref[:, cols]
        o_ref[:, cols] = (y * _silu(g_ref[:, cols].astype(F32))).astype(o_ref.dtype)


def _fnet(p3, fw, fb, *, u_col0, g_col0):
    bsz, length, _ = p3.shape
    groups, gd, _ = fw.shape
    branch = groups * gd
    ccsc = jnp.concatenate(_dft_cos_sin(gd), axis=1)
    rb = FNET_TABLE_ROWS
    t1c, t1s = _dft_rows(jnp.arange(rb, dtype=jnp.int32), length)
    t2c, t2s = _dft_rows(jnp.arange(length // rb, dtype=jnp.int32) * rb, length)
    gps = FNET_GROUPS_PER_STEP
    wide = gps * gd
    seq = lambda c0: pl.BlockSpec((None, length, wide), lambda b, g: (b, 0, c0 + g))
    table = lambda rows: pl.BlockSpec((rows, length), lambda b, g: (0, 0))
    return pl.pallas_call(
        functools.partial(_fnet_kernel, scale=float((length * gd) ** -0.5)),
        grid=(bsz, groups // gps),
        in_specs=[table(rb), table(rb), table(length // rb), table(length // rb),
                  pl.BlockSpec((gd, 2 * gd), lambda b, g: (0, 0)),
                  seq(u_col0 // wide),
                  pl.BlockSpec((gps, gd, gd), lambda b, g: (g, 0, 0)),
                  pl.BlockSpec((1, wide), lambda b, g: (0, g)),
                  seq(g_col0 // wide)],
        out_specs=seq(0),
        out_shape=jax.ShapeDtypeStruct((bsz, length, branch), BF16),
        scratch_shapes=[pltpu.VMEM((length, 2 * length), BF16)],
        compiler_params=_params("arbitrary", "arbitrary"), name="fnet",
    )(t1c, t1s, t2c, t2s, ccsc, p3, fw, fb, p3)


class _RowShift:
    def __init__(self, chunk, length):
        self.chunk, self.length, self.halo = chunk, length, BF16_SUBLANES
        self.ext = chunk + 2 * self.halo
        si = lax.broadcasted_iota(jnp.int32, (chunk, self.ext), 0)
        sj = lax.broadcasted_iota(jnp.int32, (chunk, self.ext), 1)
        self.mats = [jnp.concatenate([jnp.where(sj == si + (off - 1), 1.0, 0.0),
                                      jnp.where(sj == si + (off + 1), 1.0, 0.0)], axis=0).astype(BF16)
                     for off in (0, self.halo, 2 * self.halo)]

    def window(self, r0):
        start = pl.multiple_of(jnp.clip(r0 - self.halo, 0, self.length - self.ext), self.halo)
        off = r0 - start
        mat = jnp.where(off == self.halo, self.mats[1], jnp.where(off == 0, self.mats[0], self.mats[2]))
        return pl.ds(start, self.ext), mat


def _conv3_rows(load_rows, r0, n_rows, length, w):
    h = BF16_SUBLANES
    u = load_rows(r0, n_rows)
    prev_blk = load_rows(pl.multiple_of(jnp.maximum(r0 - h, 0), h), h)
    next_blk = load_rows(pl.multiple_of(jnp.minimum(r0 + n_rows, length - h), h), h)
    prev_row = jnp.where(r0 > 0, prev_blk[h - 1:h, :], 0.0)
    next_row = jnp.where(r0 + n_rows < length, next_blk[0:1, :], 0.0)
    ri = lax.broadcasted_iota(jnp.int32, u.shape, 0)
    up = jnp.where(ri == 0, prev_row, pltpu.roll(u, 1, 0))
    un = jnp.where(ri == n_rows - 1, next_row, pltpu.roll(u, n_rows - 1, 0))
    return up * w[0:1, :] + u * w[1:2, :] + un * w[2:3, :]


def _sconv_kernel(cin_ref, cb_ref, cc_ref, cg_ref, w_ref, o_ref, *, chunk):
    length, width = o_ref.shape

    def body(r, carry):
        r0 = pl.multiple_of(r * chunk, chunk)
        rows = pl.ds(r0, chunk)
        for j in range(width // LANES):
            cols = slice(j * LANES, (j + 1) * LANES)

            def load_u(start, size, cols=cols):
                rs = pl.ds(start, size)
                return cc_ref[rs, cols].astype(F32) * cin_ref[rs, cols].astype(F32)

            conv = _conv3_rows(load_u, r0, chunk, length, w_ref[:, cols])
            o_ref[rows, cols] = (cb_ref[rows, cols].astype(F32) * conv
                                 * _silu(cg_ref[rows, cols].astype(F32))).astype(o_ref.dtype)
        return carry

    lax.fori_loop(0, length // chunk, body, 0)


def _sconv(p3, w, *, tn):
    bsz, length, _ = p3.shape
    branch = w.shape[1]
    nb = branch // tn
    seq = lambda k: pl.BlockSpec((None, length, tn), lambda b, j, k=k: (b, 0, k * nb + j))
    return pl.pallas_call(
        functools.partial(_sconv_kernel, chunk=min(128, length)),
        grid=(bsz, nb),
        in_specs=[seq(0), seq(1), seq(2), seq(3), pl.BlockSpec((3, tn), lambda b, j: (0, j))],
        out_specs=seq(0),
        out_shape=jax.ShapeDtypeStruct((bsz, length, branch), BF16),
        compiler_params=_params("parallel", "parallel"), name="sconv",
    )(p3, p3, p3, p3, w)


def _ssd_kernel(x_ref, bm_ref, cm_ref, z_ref, dt_ref,
                wx_ref, wb_ref, wc_ref, bx_ref, bb_ref, bc_ref,
                dtb_ref, a_ref, dsk_ref,
                o_ref,
                bmt_s, cm_s, yacc_s, ecol_s, xdte_s, alast_s, st_s,
                *staging, chunk):
    length, gw = yacc_s.shape
    n_chunks = length // chunk
    heads = SSD_GROUP_HEADS
    hd = gw // heads
    xs_t, ae_t, de_t, ac_t, at_t, cb_t = zip(staging[:6], staging[6:])
    shifter = _RowShift(chunk, length)
    edge = (chunk - 1, 0)

    ri = lax.broadcasted_iota(jnp.int32, (chunk, chunk), 0)
    ci = lax.broadcasted_iota(jnp.int32, (chunk, chunk), 1)
    masks = (ri >= ci, ri <= ci)
    tris = []
    for m in masks:
        tri = jnp.where(m, 1.0, 0.0).astype(BF16)
        tris.append(jnp.concatenate([tri, tri], axis=1))
    low = lax.broadcasted_iota(jnp.int32, (chunk, LANES), 1).astype(F32).astype(BF16) < hd
    sel_l =lax.broadcasted_iota(jnp.int32, (LANES, gw), 0)
    sel_c = lax.broadcasted_iota(jnp.int32, (LANES, gw), 1) // hd
    expands = []
    for d in (0, 1):
        e = jnp.where(sel_l == sel_c + d * heads, 1.0, 0.0).astype(BF16)
        expands.append(jnp.concatenate([e, e], axis=0))
    a2_row = a_ref[...] * LOG2E

    n_pairs = gw // LANES

    def front_steps(c, slot):
        env = {}

        def conv_silu(ref, w_ref, b_ref, cols):
            both = _dot(env["shift"], ref[env["erows"], cols])
            w = w_ref[:, cols]
            y = (both[:chunk] * w[0:1, :] + ref[env["rows"], cols].astype(F32) * w[1:2, :]
                 + both[chunk:] * w[2:3, :] + b_ref[:, cols])
            return _silu(y)

        def b_conv():
            r0 = pl.multiple_of(c * chunk, chunk)
            env["rows"] = pl.ds(r0, chunk)
            env["erows"], env["shift"] = shifter.window(r0)
            env["bmt"] = bmt_c = conv_silu(bm_ref, wb_ref, bb_ref, slice(0, LANES)).T.astype(BF16)
            bmt_s[c] = bmt_c

        def c_conv():
            cm_c = conv_silu(cm_ref, wc_ref, bc_ref, slice(0, LANES)).astype(BF16)
            cm_s[env["rows"], :] = cm_c
            cb_t[slot][...] = _dot(cm_c, env["bmt"])

        def decay(d):
            if d == 0:
                to_front = (LANES - pl.program_id(1) * 2 * heads) % LANES
                t = pltpu.roll(dt_ref[env["rows"], :], to_front, 1) + dtb_ref[...]
                env["dtc"] = jnp.maximum(t, 0.0) + jnp.log1p(jnp.exp(-jnp.abs(t)))
            a = _sel_dot_rhs(tris[d], env["dtc"] * a2_row)
            ac_t[slot][d] = a
            at_t[slot][d] = a.T

        def spread(d):
            ae_t[slot][d] = _dot(jnp.concatenate(_split2(ac_t[slot][d]), axis=1), expands[d])
            de_t[slot][d] = _dot(jnp.concatenate(_split2(env["dtc"]), axis=1), expands[d])

        def x_conv(s):
            wide = gw // 2
            wcols = slice(s * wide, (s + 1) * wide)
            xs_t[slot][:, wcols] = conv_silu(x_ref, wx_ref, bx_ref, wcols)

        part = functools.partial
        return [b_conv, c_conv, part(decay, 0), part(decay, 1), part(spread, 0), part(spread, 1),
                part(x_conv, 0), part(x_conv, 1)]

    def back_unit(c, slot, p, d, acc):
        rows = pl.ds(pl.multiple_of(c * chunk, chunk), chunk)
        slab = slice(p * LANES, (p + 1) * LANES)
        xs_p = xs_t[slot][:, slab]
        if d == 0:
            acc = xs_p * dsk_ref[:, slab]
        l0 = d * heads + 2 * p
        acum_p = ae_t[slot][d, :, slab]
        xdt = xs_p * de_t[slot][d, :, slab]
        a_edge = acum_p[edge[d]:edge[d] + 1, :]
        ecol_s[d, rows, slab] = jnp.exp2(acum_p).astype(BF16)
        xdte_s[d, rows, slab] = (xdt * jnp.exp2(a_edge - acum_p)).astype(BF16)
        alast_s[d * n_chunks + c, :, slab] = jnp.broadcast_to(jnp.exp2(a_edge), (8, LANES))
        ms = []
        for l in (l0, l0 + 1):
            diff = ac_t[slot][d, :, l:l + 1] - at_t[slot][d, l:l + 1, :]
            ms.append((cb_t[slot][...] * jnp.exp2(jnp.where(masks[d], diff, -1e30))).astype(BF16))
        xb = xdt.astype(BF16)
        zero = jnp.zeros_like(xb)
        block_diag = jnp.concatenate([jnp.where(low, xb, zero), jnp.where(low, zero, xb)], axis=0)
        acc = acc + _dot(jnp.concatenate(ms, axis=1), block_diag)
        if d == 1:
            yacc_s[rows, slab] = acc
        return acc

    def prepare_two(i, carry):
        c0 = 2 * i
        for c_back, s_back, c_front, s_front in ((c0, 0, c0 + 1, 1),
                                                 (c0 + 1, 1, jnp.minimum(c0 + 2, n_chunks - 1), 0)):
            steps = front_steps(c_front, s_front)
            acc = None
            for p in range(n_pairs):
                for d in (0, 1):
                    acc = back_unit(c_back, s_back, p, d, acc)
                    steps[2 * p + d]()
        return carry

    for step in front_steps(0, 0):
        step()
    lax.fori_loop(0, n_chunks // 2, prepare_two, 0, unroll=min(SSD_PREPARE_UNROLL, n_chunks // 2))

    st_s[...] = jnp.zeros_like(st_s)

    def scan(c, carry, *, finalize):
        for d in (0, 1):
            cc = c if d == 0 else n_chunks - 1 - c
            rows = pl.ds(pl.multiple_of(cc * chunk, chunk), chunk)
            state = st_s[d]
            y = _dot(cm_s[rows, :], state.astype(BF16)) * ecol_s[d, rows, :].astype(F32)
            st_s[d] = state * alast_s[d * n_chunks + cc][0:1, :] + _dot(bmt_s[cc], xdte_s[d, rows, :])
            if finalize:
                tot = yacc_s[rows, :] + y
                o_ref[rows, :] = (tot * _silu(z_ref[rows, :].astype(F32))).astype(o_ref.dtype)
            else:
                yacc_s[rows, :] = yacc_s[rows, :] + y
        return carry

    half = n_chunks // 2
    unroll = min(SSD_SCAN_UNROLL, half)
    lax.fori_loop(0, half, functools.partial(scan, finalize=False), 0, unroll=unroll)
    lax.fori_loop(half, n_chunks, functools.partial(scan, finalize=True), 0, unroll=unroll)


def _ssd(p3, bc3, dt3, conv_w, conv_b, dt_bias_g, a_g, dsk, *, x_col0, z_col0, chunk):
    bsz, length, _ = p3.shape
    gw, ns = SSD_GROUP_W, SSD_STATE
    branch = dsk.shape[1]
    groups = branch // gw
    xb, bb, cb = x_col0 // gw, 0, groups
    seq =lambda w, c0: pl.BlockSpec((None, length, w), lambda b, g: (b, 0, c0 + g))
    par = lambda rows, w, c0: pl.BlockSpec((rows, w), lambda b, g: (0, c0 + g))
    n_chunks = length // chunk
    return pl.pallas_call(
        functools.partial(_ssd_kernel, chunk=chunk),
        grid=(bsz, groups),
        in_specs=[seq(gw, xb), seq(ns, bb), seq(ns, cb), seq(gw, z_col0 // gw),
                  pl.BlockSpec((None, length, LANES), lambda b, g: (b, 0, 0)),
                  par(3, gw, 0), par(3, ns, branch // ns), par(3, ns, (branch + groups * ns) // ns),
                  par(1, gw, 0), par(1, ns, branch // ns), par(1, ns, (branch + groups * ns) // ns),
                  par(1, LANES, 0), par(1, LANES, 0),
                  par(1, gw, 0)],
        out_specs=seq(gw, 0),
        out_shape=jax.ShapeDtypeStruct((bsz, length, branch), BF16),
        scratch_shapes=[pltpu.VMEM((n_chunks, ns, chunk), BF16),
                        pltpu.VMEM((length, ns), BF16),
                        pltpu.VMEM((length, gw), F32),
                        pltpu.VMEM((2, length, gw), BF16),
                        pltpu.VMEM((2, length, gw), BF16),
                        pltpu.VMEM((2 * n_chunks, 8, gw), F32),
                        pltpu.VMEM((2, ns, gw), F32),
                        ] + 2 * [
                        pltpu.VMEM((chunk, gw), F32),
                        pltpu.VMEM((2, chunk, gw), F32),
                        pltpu.VMEM((2, chunk, gw), F32),
                        pltpu.VMEM((2, chunk, LANES), F32),
                        pltpu.VMEM((2, LANES, chunk), F32),
                        pltpu.VMEM((chunk, chunk), F32)],
        compiler_params=_params("parallel", "parallel"), name="ssd",
    )(p3, bc3, bc3, p3, dt3, conv_w, conv_w, conv_w, conv_b, conv_b, conv_b,
      dt_bias_g, a_g, dsk)


def _ssd_group_lanes(v, groups):
    hpg = v.shape[1] // groups
    per = jnp.concatenate([v[0].reshape(groups, hpg), v[1].reshape(groups, hpg)], axis=1)
    per = jnp.pad(per, ((0, 0), (0, LANES - 2 * hpg)))
    return per.reshape(1, groups * LANES)


def kernel(x, norm_w, final_norm_w, ev_w_in, ev_w_out, hgrn_lb_logits, hgrn_norm_w, fnet_w, fnet_b,
           od_w_in, od_w_out, sconv_w, ssd_conv_w, ssd_conv_b, ssd_dt_bias, ssd_a_log, ssd_d, ssd_norm_w):
    bsz, length, d = x.shape
    t = bsz * length
    branch = d
    x2 = x.reshape(t, d)
    row = lambda v: v.astype(F32).reshape(1, -1)

    lower_bounds = jnp.cumsum(jax.nn.softmax(hgrn_lb_logits.astype(F32), axis=0), axis=0)
    p = _inproj(x2, row(norm_w[0]), ev_w_in[0].astype(BF16), n=ev_w_in.shape[2], tm=1024, tn=2048)
    p3 = p.reshape(bsz, length, -1)
    a_out = _hgrn(p3, row(lower_bounds[0]), row(hgrn_norm_w[0]), col0=0, chunk=HGRN_CHUNK)
    b_out = _fnet(p3, fnet_w[0].astype(BF16), row(fnet_b[0]), u_col0=5 * branch, g_col0=6 * branch)
    x2 = _outproj(a_out.reshape(t, branch), b_out.reshape(t, branch), x2,
                  ev_w_out[0].astype(BF16), tm=512)

    heads = ssd_d.shape[1]
    groups = heads // SSD_GROUP_HEADS
    n_main = 5 * branch + branch + 2 * groups * SSD_STATE
    w_in = od_w_in[0]
    w_dt = w_in[:, n_main:].reshape(d, 2, groups, SSD_GROUP_HEADS)
    w_dt = jnp.transpose(w_dt, (0, 2, 1, 3)).reshape(d, 2 * heads)
    w_dt = jnp.pad(w_dt, ((0, 0), (0, LANES - 2 * heads)))
    w_main = _cast_transposed(w_in.T, n_main, tn=1024)
    n_bc = 2 * groups * SSD_STATE
    n_wide = n_main - n_bc
    p, h = _inproj(x2, row(norm_w[1]), w_main, n=n_wide, tm=1024, tn=2048, keep_h=True)
    bc, dt = _proj_tail(h, w_main, w_dt.astype(BF16), col0=n_wide, n=n_bc, tm=1024)
    p3 = p.reshape(bsz, length, n_wide)
    c_out = _sconv(p3, sconv_w[0].astype(F32), tn=512)
    a_neg = -jnp.exp(ssd_a_log[0].astype(F32))
    d_out = _ssd(p3, bc.reshape(bsz, length, n_bc), dt.reshape(bsz, length, LANES),
                 ssd_conv_w[0].astype(F32), row(ssd_conv_b[0]),
                 _ssd_group_lanes(ssd_dt_bias[0].astype(F32), groups),
                 _ssd_group_lanes(a_neg, groups),
                 row(jnp.repeat(ssd_d[0].astype(F32), SSD_HEAD)),
                 x_col0=5 * branch, z_col0=4 * branch, chunk=SSD_CHUNK)
    out = _outproj(c_out.reshape(t, branch), d_out.reshape(t, branch), x2,
                   od_w_out[0].astype(BF16), row(ssd_norm_w[0]), row(final_norm_w), tm=512)
    return out.reshape(bsz, length, d)
```

```python
import functools

import jax
import jax.numpy as jnp
from jax import lax
from jax.experimental import pallas as pl
from jax.experimental.pallas import tpu as pltpu

F32 = jnp.float32
BF16 = jnp.bfloat16
EPS = 1e-6
LOG2E = 1.4426950408889634

VMEM_LIMIT_BYTES = 56 * 1024 * 1024
LANES = 128
BF16_SUBLANES = 16

PROJ_ROW_TILE = 1024
PROJ_COL_TILE = 2048
CAST_COL_TILE = 1024
OUTPROJ_ROW_TILE = 512
SCONV_COL_TILE = 512
SCONV_CHUNK = 128
NORM_ROWS = 128

HGRN_HEAD = 128
HGRN_CHUNK = 64
HGRN_BLOCK = 128
HGRN_SCAN_UNROLL = 16
FNET_TABLE_ROWS = 64
FNET_K_CHUNK = 1024
SSD_HEAD = 64
SSD_STATE = 128
SSD_GROUP_HEADS = 8
SSD_GROUP_W = SSD_HEAD * SSD_GROUP_HEADS
SSD_CHUNK = 128
SSD_SCAN_UNROLL = 8
SSD_PREPARE_UNROLL = 8


def _params(*sem):
    return pltpu.CompilerParams(dimension_semantics=sem, vmem_limit_bytes=VMEM_LIMIT_BYTES)


def _dot(a, b):
    return jnp.dot(a, b, preferred_element_type=F32)


def _dot_nt(a, b):
    return lax.dot_general(a, b, (((1,), (1,)), ((), ())), preferred_element_type=F32)


def _dot_tn(a, b):
    return lax.dot_general(a, b, (((0,), (0,)), ((), ())), preferred_element_type=F32)


def _split2(x):
    hi = x.astype(BF16)
    lo = (x - hi.astype(F32)).astype(BF16)
    return hi, lo


def _sel_dot_rhs(sel_sel, x):
    return _dot(sel_sel, jnp.concatenate(_split2(x), axis=0))


def _sel_dot_lhs(x, sel_sel):
    return _dot(jnp.concatenate(_split2(x), axis=1), sel_sel)


def _silu(x):
    return x * jax.nn.sigmoid(x)


def _rms_scale(x):
    return lax.rsqrt(jnp.mean(x * x, axis=-1, keepdims=True) + EPS)


def _cast_t_kernel(wt_ref, o_ref):
    o_ref[...] = wt_ref[...].T.astype(o_ref.dtype)


def _cast_transposed(wt, n, *, tn):
    d = wt.shape[1]
    return pl.pallas_call(
        _cast_t_kernel, grid=(n // tn,),
        in_specs=[pl.BlockSpec((tn, d), lambda j: (j, 0))],
        out_specs=pl.BlockSpec((d, tn), lambda j: (0, j)),
        out_shape=jax.ShapeDtypeStruct((d, n), BF16),
        compiler_params=_params("parallel"), name="cast_t",
    )(wt)


def _norm_rows_to(h_ref, x_ref, nw_ref, row0, n_rows, rows_per_step):
    nw = nw_ref[...]
    for r in range(row0, row0 + n_rows, rows_per_step):
        x = x_ref[r:r + rows_per_step, :]
        h_ref[r:r + rows_per_step, :] = (x * _rms_scale(x) * nw).astype(BF16)


def _inproj_kernel(x_ref, nw_ref, w_ref, o_ref, h_ref):
    tm = x_ref.shape[0]
    first = pl.program_id(1) == 0

    @pl.when(first)
    def _():
        half = tm // 2
        for row0 in (0, half):
            _norm_rows_to(h_ref, x_ref, nw_ref, row0, half, min(NORM_ROWS, half))
            rows = slice(row0, row0 + half)
            o_ref[rows, :] = _dot(h_ref[rows, :], w_ref[...]).astype(o_ref.dtype)

    @pl.when(jnp.logical_not(first))
    def _():
        o_ref[...] = _dot(h_ref[...], w_ref[...]).astype(o_ref.dtype)


def _inproj(x2, nw, w, *, n, tm, tn, keep_h=False):
    t, d = x2.shape
    tm = min(tm, t)
    grid = (t // tm, n // tn)
    in_specs = [
        pl.BlockSpec((tm, d), lambda i, j: (i, 0)),
        pl.BlockSpec((1, d), lambda i, j: (0, 0)),
        pl.BlockSpec((d, tn), lambda i, j: (0, j)),
    ]
    out_main = pl.BlockSpec((tm, tn), lambda i, j: (i, j))
    if not keep_h:
        return pl.pallas_call(
            _inproj_kernel, grid=grid, in_specs=in_specs, out_specs=out_main,
            out_shape=jax.ShapeDtypeStruct((t, n), BF16), scratch_shapes=[pltpu.VMEM((tm, d), BF16)],
            compiler_params=_params("parallel", "arbitrary"), name="inproj",
        )(x2, nw, w)
    return pl.pallas_call(
        _inproj_kernel, grid=grid, in_specs=in_specs,
        out_specs=[out_main, pl.BlockSpec((tm, d), lambda i, j: (i, 0))],
        out_shape=[jax.ShapeDtypeStruct((t, n), BF16), jax.ShapeDtypeStruct((t, d), BF16)],
        compiler_params=_params("parallel", "arbitrary"), name="inproj_keep",
    )(x2, nw, w)


def _proj_tail_kernel(h_ref, w_ref, wdt_ref, o_ref, dt_ref):
    h = h_ref[...]
    o_ref[...] = _dot(h, w_ref[...]).astype(o_ref.dtype)
    dt_ref[...] = _dot(h, wdt_ref[...])


def _proj_tail(h, w, wdt, *, col0, n, tm):
    t, d = h.shape
    tm = min(tm, t)
    ndt = wdt.shape[1]
    rows = lambda width: pl.BlockSpec((tm, width), lambda i: (i, 0))
    return pl.pallas_call(
        _proj_tail_kernel, grid=(t // tm,),
        in_specs=[rows(d), pl.BlockSpec((d, n), lambda i: (0, col0 // n)), pl.BlockSpec((d, ndt), lambda i: (0, 0))],
        out_specs=[rows(n), rows(ndt)],
        out_shape=[jax.ShapeDtypeStruct((t, n), BF16), jax.ShapeDtypeStruct((t, ndt), F32)],
        compiler_params=_params("parallel"), name="proj_tail",
    )(h, w, wdt)


def _outproj_kernel(*refs, norm_b, final_norm):
    a_ref, b_ref, x_ref, w_ref = refs[:4]
    rest = list(refs[4:])
    o_ref = rest.pop()
    ka = a_ref.shape[1]
    bnw_ref = rest.pop(0) if norm_b else None
    fnw_ref = rest.pop(0) if final_norm else None
    b = b_ref[...]
    if norm_b:
        bf = b.astype(F32)
        b = (bf * _rms_scale(bf) * bnw_ref[...]).astype(BF16)
    acc = _dot(a_ref[...], w_ref[0:ka, :]) + _dot(b, w_ref[ka:, :]) + x_ref[...]
    if final_norm:
        acc = acc * _rms_scale(acc) * fnw_ref[...]
    o_ref[...] = acc


def _outproj(a, b, x2, w, b_norm_w=None, final_norm_w=None, *, tm):
    t, d = x2.shape
    ka, kb = a.shape[1], b.shape[1]
    tm = min(tm, t)
    row = lambda width: pl.BlockSpec((tm, width), lambda i: (i, 0))
    const = lambda shape: pl.BlockSpec(shape, lambda i: (0, 0))
    in_specs = [row(ka), row(kb), row(d), const((ka + kb, d))]
    args = [a, b, x2, w]
    if b_norm_w is not None:
        in_specs.append(const((1, kb)))
        args.append(b_norm_w)
    if final_norm_w is not None:
        in_specs.append(const((1, d)))
        args.append(final_norm_w)
    kern = functools.partial(_outproj_kernel, norm_b=b_norm_w is not None,
                             final_norm=final_norm_w is not None)
    return pl.pallas_call(
        kern, grid=(t // tm,), in_specs=in_specs, out_specs=row(d),
        out_shape=jax.ShapeDtypeStruct((t, d), F32),
        compiler_params=_params("parallel"), name="outproj",
    )(*args)


def _hgrn_kernel(q_ref, v_ref, zf_ref, zb_ref, g_ref, lb_ref, nw_ref, o_ref,
                 qin_s, kin_s, qdec_s, kendt_s, v_s, dec_s, acc_s, *, chunk, block, n_heads):
    length, width = acc_s.shape
    n_chunks = length // chunk
    n_blocks = length // block
    per_block = block // chunk
    mid = chunk // 2
    z_refs = (zf_ref, zb_ref)
    edge = (chunk - 1, 0)

    def prepare(blk, st):
        lb = lb_ref[...]
        one_m_lb = 1.0 - lb
        bri = lax.broadcasted_iota(jnp.int32, (block, block), 0)
        bci = lax.broadcasted_iota(jnp.int32, (block, block), 1)
        same_chunk = (bri // chunk) == (bci // chunk)
        r0 = pl.multiple_of(blk * block, block)
        rows = pl.ds(r0, block)
        q = q_ref[rows, :].astype(F32)
        v_s[st, rows, :] = v_ref[rows, :]
        for d in (0, 1):
            tri = jnp.where(same_chunk & ((bri <= bci) if d else (bri >= bci)), 1.0, 0.0).astype(BF16)
            z = z_refs[d][rows, :].astype(F32)
            sg = jax.nn.sigmoid(z)
            logf = jnp.log2(lb + one_m_lb * sg)
            key = one_m_lb * (1.0 - sg)
            b2 = _dot(tri, jnp.concatenate(_split2(logf), axis=1))
            b = b2[:, :width] + b2[:, width:]
            for k in range(per_block):
                sl = slice(k * chunk, (k + 1) * chunk)
                rk = pl.ds(r0 + k * chunk, chunk)
                bk = b[sl]
                b_mid = bk[mid:mid + 1, :]
                b_edge = bk[edge[d]:edge[d] + 1, :]
                q_in = q[sl] * jnp.exp2(bk - b_mid)
                k_in = key[sl] * jnp.exp2(b_mid - bk)
                qin_s[st, d, rk, :] = q_in.astype(BF16)
                kin_s[st, d, rk, :] = k_in.astype(BF16)
                qdec_s[st, d, rk, :] = (q_in * jnp.exp2(b_mid)).astype(BF16)
                idx = d * n_chunks + blk * per_block + k
                kendt_s[st, idx] = (k_in * jnp.exp2(b_edge - b_mid)).T.astype(BF16)
                dec_s[st, idx] = jnp.broadcast_to(jnp.exp2(b_edge), (width, width)).T

    def scan(c, states, st, finalize):
        ri = lax.broadcasted_iota(jnp.int32, (chunk, chunk), 0)
        ci = lax.broadcasted_iota(jnp.int32, (chunk, chunk), 1)
        new_states = []
        for d in (0, 1):
            cc = c if d == 0 else n_chunks - 1 - c
            rows = pl.ds(pl.multiple_of(cc * chunk, chunk), chunk)
            idx = d * n_chunks + cc
            v = v_s[st, rows, :]
            s = _dot_nt(qin_s[st, d, rows, :], kin_s[st, d, rows, :])
            s = jnp.where((ri <= ci) if d else (ri >= ci), s, 0.0).astype(BF16)
            o = _dot(jnp.concatenate([qdec_s[st, d, rows, :], s], axis=1),
                     jnp.concatenate([states[d].astype(BF16), v], axis=0))
            new_states.append(states[d] * dec_s[st, idx] + _dot(kendt_s[st, idx], v))
            if finalize:
                tot = acc_s[rows, :] + o
                g = g_ref[rows, :].astype(F32)
                o_ref[rows, :] = (tot * _rms_scale(tot) * nw_ref[...] * _silu(g)).astype(o_ref.dtype)
            else:
                acc_s[rows, :] = o
        return tuple(new_states)

    half = n_chunks // 2
    unroll = min(HGRN_SCAN_UNROLL, half)
    trips = half // unroll
    blocks_per_trip = n_blocks // (2 * trips)

    def run(prep_set, scan_set):
        def trip(i, states, *, finalize):
            if scan_set is not None:
                for u in range(unroll):
                    states = scan(i * unroll + u, states, scan_set, finalize)
            if prep_set is not None:
                for u in range(blocks_per_trip):
                    prepare(i * blocks_per_trip + u, prep_set)
            return states

        zero = jnp.zeros((width, width), F32)
        states = lax.fori_loop(0, trips, functools.partial(trip, finalize=False), (zero, zero))
        lax.fori_loop(trips, 2 * trips, functools.partial(trip, finalize=True), states)

    step = pl.program_id(0)
    inner = (step > 0) & (step < n_heads)
    pl.when(step == 0)(lambda: run(0, None))
    pl.when(inner & (step % 2 == 1))(lambda: run(1, 0))
    pl.when(inner & (step % 2 == 0))(lambda: run(0, 1))
    pl.when(step == n_heads)(lambda: run(None, (n_heads - 1) % 2))


def _hgrn(p3, lb, nw, *, col0, chunk):
    bsz, length, _ = p3.shape
    width = HGRN_HEAD
    branch = lb.shape[1]
    heads = branch // width
    n_heads = bsz * heads
    c0 = col0 // width
    n_chunks = length // chunk
    cur = lambda s: jnp.minimum(s, n_heads - 1)
    prev = lambda s: jnp.maximum(s - 1, 0)
    seq = lambda k, at: pl.BlockSpec((None, length, width),
                                     lambda s: (at(s) // heads, 0, c0 + k * heads + at(s) % heads))
    vec = lambda at: pl.BlockSpec((1, width), lambda s: (0, at(s) % heads))
    scaled = pltpu.VMEM((2, 2, length, width), BF16)
    return pl.pallas_call(
        functools.partial(_hgrn_kernel, chunk=chunk, block=min(HGRN_BLOCK, length), n_heads=n_heads),
        grid=(n_heads + 1,),
        in_specs=[seq(0, cur), seq(1, cur), seq(2, cur), seq(3, cur), seq(4, prev), vec(cur), vec(prev)],
        out_specs=pl.BlockSpec((None, length, width), lambda s: (prev(s) // heads, 0, prev(s) % heads)),
        out_shape=jax.ShapeDtypeStruct((bsz, length, branch), BF16),
        scratch_shapes=[scaled, scaled, scaled,
                        pltpu.VMEM((2, 2 * n_chunks, width, chunk), BF16),
                        pltpu.VMEM((2, length, width), BF16),
                        pltpu.VMEM((2, 2 * n_chunks, width, width), F32),
                        pltpu.VMEM((length, width), F32)],
        compiler_params=_params("arbitrary"), name="hgrn2",
    )(p3, p3, p3, p3, p3, lb, nw)


def _dft_cos_sin(n):
    k = lax.broadcasted_iota(jnp.int32, (n, n), 0) * lax.broadcasted_iota(jnp.int32, (n, n), 1)
    ang = (k % n).astype(F32) * (2.0 * jnp.pi / n)
    return jnp.cos(ang).astype(BF16), jnp.sin(ang).astype(BF16)


def _dft_rows(rows, n):
    k = rows[:, None] * lax.broadcasted_iota(jnp.int32, (rows.shape[0], n), 1)
    ang = (k % n).astype(F32) * (2.0 * jnp.pi / n)
    return jnp.cos(ang), jnp.sin(ang)


def _fnet_kernel(t1c_ref, t1s_ref, t2c_ref, t2s_ref, ccsc_ref, u_ref, fw_ref, fb_ref, g_ref, o_ref,
                 cs_s, *, scale):
    rb = t1c_ref.shape[0]
    length, gd = u_ref.shape

    @pl.when((pl.program_id(0) == 0) & (pl.program_id(1) == 0))
    def _():
        def gen(a, carry):
            rows = pl.ds(pl.multiple_of(a * rb, rb), rb)
            c2, s2 = t2c_ref[pl.ds(a, 1), :], t2s_ref[pl.ds(a, 1), :]
            c1, s1 = t1c_ref[...], t1s_ref[...]
            cs_s[rows, :length] = (c2 * c1 - s2 * s1).astype(BF16)
            cs_s[rows, length:] = (-(s2 * c1 + c2 * s1)).astype(BF16)
            return carry

        lax.fori_loop(0, t2c_ref.shape[0], gen, 0)

    z = _dot(u_ref[...], ccsc_ref[...])
    zz = jnp.concatenate([z[:, :gd], z[:, gd:]], axis=0).astype(BF16)
    kc = min(FNET_K_CHUNK, 2 * length)
    mixed = _dot(cs_s[:, :kc], zz[:kc])
    for k0 in range(kc, 2 * length, kc):
        mixed = mixed + _dot(cs_s[:, k0:k0 + kc], zz[k0:k0 + kc])
    y = _dot((mixed * scale).astype(BF16), fw_ref[...]) + fb_ref[...]
    o_ref[...] = (y * _silu(g_ref[...].astype(F32))).astype(o_ref.dtype)


def _fnet(p3, fw, fb, *, u_col0, g_col0):
    bsz, length, _ = p3.shape
    groups, gd, _ = fw.shape
    branch = groups * gd
    ccsc = jnp.concatenate(_dft_cos_sin(gd), axis=1)
    rb = FNET_TABLE_ROWS
    t1c, t1s = _dft_rows(jnp.arange(rb, dtype=jnp.int32), length)
    t2c, t2s = _dft_rows(jnp.arange(length // rb, dtype=jnp.int32) * rb, length)
    seq = lambda c0: pl.BlockSpec((None, length, gd), lambda b, g: (b, 0, c0 + g))
    table = lambda rows: pl.BlockSpec((rows, length), lambda b, g: (0, 0))
    return pl.pallas_call(
        functools.partial(_fnet_kernel, scale=float((length * gd) ** -0.5)),
        grid=(bsz, groups),
        in_specs=[table(rb), table(rb), table(length // rb), table(length // rb),
                  pl.BlockSpec((gd, 2 * gd), lambda b, g: (0, 0)),
                  seq(u_col0 // gd),
                  pl.BlockSpec((None, gd, gd), lambda b, g: (g, 0, 0)),
                  pl.BlockSpec((1, gd), lambda b, g: (0, g)),
                  seq(g_col0 // gd)],
        out_specs=seq(0),
        out_shape=jax.ShapeDtypeStruct((bsz, length, branch), BF16),
        scratch_shapes=[pltpu.VMEM((length, 2 * length), BF16)],
        compiler_params=_params("arbitrary", "arbitrary"), name="fnet",
    )(t1c, t1s, t2c, t2s, ccsc, p3, fw, fb, p3)


class _RowShift:
    def __init__(self, chunk, length):
        self.chunk, self.length, self.halo = chunk, length, BF16_SUBLANES
        self.ext = chunk + 2 * self.halo
        si = lax.broadcasted_iota(jnp.int32, (chunk, self.ext), 0)
        sj = lax.broadcasted_iota(jnp.int32, (chunk, self.ext), 1)
        self.mats = [jnp.concatenate([jnp.where(sj == si + (off - 1), 1.0, 0.0),
                                      jnp.where(sj == si + (off + 1), 1.0, 0.0)], axis=0).astype(BF16)
                     for off in (0, self.halo, 2 * self.halo)]

    def window(self, r0):
        start = pl.multiple_of(jnp.clip(r0 - self.halo, 0, self.length - self.ext), self.halo)
        off = r0 - start
        mat = jnp.where(off == self.halo, self.mats[1], jnp.where(off == 0, self.mats[0], self.mats[2]))
        return pl.ds(start, self.ext), mat


def _conv3_rows(load_rows, r0, n_rows, length, w):
    h = BF16_SUBLANES
    u = load_rows(r0, n_rows)
    prev_blk = load_rows(pl.multiple_of(jnp.maximum(r0 - h, 0), h), h)
    next_blk = load_rows(pl.multiple_of(jnp.minimum(r0 + n_rows, length - h), h), h)
    prev_row = jnp.where(r0 > 0, prev_blk[h - 1:h, :], 0.0)
    next_row = jnp.where(r0 + n_rows < length, next_blk[0:1, :], 0.0)
    ri = lax.broadcasted_iota(jnp.int32, u.shape, 0)
    up = jnp.where(ri == 0, prev_row, pltpu.roll(u, 1, 0))
    un = jnp.where(ri == n_rows - 1, next_row, pltpu.roll(u, n_rows - 1, 0))
    return up * w[0:1, :] + u * w[1:2, :] + un * w[2:3, :]


def _sconv_kernel(cin_ref, cb_ref, cc_ref, cg_ref, w_ref, o_ref, *, chunk):
    length, width = o_ref.shape

    def body(r, carry):
        r0 = pl.multiple_of(r * chunk, chunk)
        rows = pl.ds(r0, chunk)
        for j in range(width // LANES):
            cols = slice(j * LANES, (j + 1) * LANES)

            def load_u(start, size, cols=cols):
                rs = pl.ds(start, size)
                return cc_ref[rs, cols].astype(F32) * cin_ref[rs, cols].astype(F32)

            conv = _conv3_rows(load_u, r0, chunk, length, w_ref[:, cols])
            o_ref[rows, cols] = (cb_ref[rows, cols].astype(F32) * conv
                                 * _silu(cg_ref[rows, cols].astype(F32))).astype(o_ref.dtype)
        return carry

    lax.fori_loop(0, length // chunk, body, 0)


def _sconv(p3, w, *, tn):
    bsz, length, _ = p3.shape
    branch = w.shape[1]
    nb = branch // tn
    seq = lambda k: pl.BlockSpec((None, length, tn), lambda b, j, k=k: (b, 0, k * nb + j))
    return pl.pallas_call(
        functools.partial(_sconv_kernel, chunk=min(SCONV_CHUNK, length)),
        grid=(bsz, nb),
        in_specs=[seq(0), seq(1), seq(2), seq(3), pl.BlockSpec((3, tn), lambda b, j: (0, j))],
        out_specs=seq(0),
        out_shape=jax.ShapeDtypeStruct((bsz, length, branch), BF16),
        compiler_params=_params("parallel", "parallel"), name="sconv",
    )(p3, p3, p3, p3, w)


def _ssd_kernel(x_ref, bm_ref, cm_ref, z_ref, dt_ref,
                wx_ref, wb_ref, wc_ref, bx_ref, bb_ref, bc_ref,
                dtb_ref, a_ref, dsk_ref,
                o_ref,
                bmt_s, cm_s, yacc_s, ecol_s, xdte_s, alast_s, st_s,
                *staging, chunk):
    length, gw = yacc_s.shape
    n_chunks = length // chunk
    heads = SSD_GROUP_HEADS
    hd = gw // heads
    xs_t, ae_t, de_t, ac_t, at_t, cb_t = zip(staging[:6], staging[6:])
    shifter = _RowShift(chunk, length)
    edge = (chunk - 1, 0)

    ri = lax.broadcasted_iota(jnp.int32, (chunk, chunk), 0)
    ci = lax.broadcasted_iota(jnp.int32, (chunk, chunk), 1)
    masks = (ri >= ci, ri <= ci)
    tris = []
    for m in masks:
        tri = jnp.where(m, 1.0, 0.0).astype(BF16)
        tris.append(jnp.concatenate([tri, tri], axis=1))
    low = lax.broadcasted_iota(jnp.int32, (chunk, LANES), 1).astype(F32).astype(BF16) < hd
    sel_l =lax.broadcasted_iota(jnp.int32, (LANES, gw), 0)
    sel_c = lax.broadcasted_iota(jnp.int32, (LANES, gw), 1) // hd
    expands = []
    for d in (0, 1):
        e = jnp.where(sel_l == sel_c + d * heads, 1.0, 0.0).astype(BF16)
        expands.append(jnp.concatenate([e, e], axis=0))
    a2_row = a_ref[...] * LOG2E

    n_pairs = gw // LANES

    def front_steps(c, slot):
        env = {}

        def conv_silu(ref, w_ref, b_ref, cols):
            both = _dot(env["shift"], ref[env["erows"], cols])
            w = w_ref[:, cols]
            y = (both[:chunk] * w[0:1, :] + ref[env["rows"], cols].astype(F32) * w[1:2, :]
                 + both[chunk:] * w[2:3, :] + b_ref[:, cols])
            return _silu(y)

        def b_conv():
            r0 = pl.multiple_of(c * chunk, chunk)
            env["rows"] = pl.ds(r0, chunk)
            env["erows"], env["shift"] = shifter.window(r0)
            env["bmt"] = bmt_c = conv_silu(bm_ref, wb_ref, bb_ref, slice(0, LANES)).T.astype(BF16)
            bmt_s[c] = bmt_c

        def c_conv():
            cm_c = conv_silu(cm_ref, wc_ref, bc_ref, slice(0, LANES)).astype(BF16)
            cm_s[env["rows"], :] = cm_c
            cb_t[slot][...] = _dot(cm_c, env["bmt"])

        def decay(d):
            if d == 0:
                to_front = (LANES - pl.program_id(1) * 2 * heads) % LANES
                t = pltpu.roll(dt_ref[env["rows"], :], to_front, 1) + dtb_ref[...]
                env["dtc"] = jnp.maximum(t, 0.0) + jnp.log1p(jnp.exp(-jnp.abs(t)))
            a = _sel_dot_rhs(tris[d], env["dtc"] * a2_row)
            ac_t[slot][d] = a
            at_t[slot][d] = a.T

        def spread(d):
            ae_t[slot][d] = _dot(jnp.concatenate(_split2(ac_t[slot][d]), axis=1), expands[d])
            de_t[slot][d] = _dot(jnp.concatenate(_split2(env["dtc"]), axis=1), expands[d])

        def x_conv(s):
            wide = gw // 2
            wcols = slice(s * wide, (s + 1) * wide)
            xs_t[slot][:, wcols] = conv_silu(x_ref, wx_ref, bx_ref, wcols)

        part = functools.partial
        return [b_conv, c_conv, part(decay, 0), part(decay, 1), part(spread, 0), part(spread, 1),
                part(x_conv, 0), part(x_conv, 1)]

    def back_unit(c, slot, p, d, acc):
        rows = pl.ds(pl.multiple_of(c * chunk, chunk), chunk)
        slab = slice(p * LANES, (p + 1) * LANES)
        xs_p = xs_t[slot][:, slab]
        if d == 0:
            acc = xs_p * dsk_ref[:, slab]
        l0 = d * heads + 2 * p
        acum_p = ae_t[slot][d, :, slab]
        xdt = xs_p * de_t[slot][d, :, slab]
        a_edge = acum_p[edge[d]:edge[d] + 1, :]
        ecol_s[d, rows, slab] = jnp.exp2(acum_p).astype(BF16)
        xdte_s[d, rows, slab] = (xdt * jnp.exp2(a_edge - acum_p)).astype(BF16)
        alast_s[d * n_chunks + c, :, slab] = jnp.broadcast_to(jnp.exp2(a_edge), (8, LANES))
        ms = []
        for l in (l0, l0 + 1):
            diff = ac_t[slot][d, :, l:l + 1] - at_t[slot][d, l:l + 1, :]
            ms.append((cb_t[slot][...] * jnp.exp2(jnp.where(masks[d], diff, -1e30))).astype(BF16))
        xb = xdt.astype(BF16)
        zero = jnp.zeros_like(xb)
        block_diag = jnp.concatenate([jnp.where(low, xb, zero), jnp.where(low, zero, xb)], axis=0)
        acc = acc + _dot(jnp.concatenate(ms, axis=1), block_diag)
        if d == 1:
            yacc_s[rows, slab] = acc
        return acc

    def prepare_two(i, carry):
        c0 = 2 * i
        for c_back, s_back, c_front, s_front in ((c0, 0, c0 + 1, 1),
                                                 (c0 + 1, 1, jnp.minimum(c0 + 2, n_chunks - 1), 0)):
            steps = front_steps(c_front, s_front)
            acc = None
            for p in range(n_pairs):
                for d in (0, 1):
                    acc = back_unit(c_back, s_back, p, d, acc)
                    steps[2 * p + d]()
        return carry

    for step in front_steps(0, 0):
        step()
    lax.fori_loop(0, n_chunks // 2, prepare_two, 0, unroll=min(SSD_PREPARE_UNROLL, n_chunks // 2))

    st_s[...] = jnp.zeros_like(st_s)

    def scan(c, carry, *, finalize):
        for d in (0, 1):
            cc = c if d == 0 else n_chunks - 1 - c
            rows = pl.ds(pl.multiple_of(cc * chunk, chunk), chunk)
            state = st_s[d]
            y = _dot(cm_s[rows, :], state.astype(BF16)) * ecol_s[d, rows, :].astype(F32)
            st_s[d] = state * alast_s[d * n_chunks + cc][0:1, :] + _dot(bmt_s[cc], xdte_s[d, rows, :])
            if finalize:
                tot = yacc_s[rows, :] + y
                o_ref[rows, :] = (tot * _silu(z_ref[rows, :].astype(F32))).astype(o_ref.dtype)
            else:
                yacc_s[rows, :] = yacc_s[rows, :] + y
        return carry

    half = n_chunks // 2
    unroll = min(SSD_SCAN_UNROLL, half)
    lax.fori_loop(0, half, functools.partial(scan, finalize=False), 0, unroll=unroll)
    lax.fori_loop(half, n_chunks, functools.partial(scan, finalize=True), 0, unroll=unroll)


def _ssd(p3, bc3, dt3, conv_w, conv_b, dt_bias_g, a_g, dsk, *, x_col0, z_col0, chunk):
    bsz, length, _ = p3.shape
    gw, ns = SSD_GROUP_W, SSD_STATE
    branch = dsk.shape[1]
    groups = branch // gw
    xb, bb, cb = x_col0 // gw, 0, groups
    seq =lambda w, c0: pl.BlockSpec((None, length, w), lambda b, g: (b, 0, c0 + g))
    par = lambda rows, w, c0: pl.BlockSpec((rows, w), lambda b, g: (0, c0 + g))
    n_chunks = length // chunk
    return pl.pallas_call(
        functools.partial(_ssd_kernel, chunk=chunk),
        grid=(bsz, groups),
        in_specs=[seq(gw, xb), seq(ns, bb), seq(ns, cb), seq(gw, z_col0 // gw),
                  pl.BlockSpec((None, length, LANES), lambda b, g: (b, 0, 0)),
                  par(3, gw, 0), par(3, ns, branch // ns), par(3, ns, (branch + groups * ns) // ns),
                  par(1, gw, 0), par(1, ns, branch // ns), par(1, ns, (branch + groups * ns) // ns),
                  par(1, LANES, 0), par(1, LANES, 0),
                  par(1, gw, 0)],
        out_specs=seq(gw, 0),
        out_shape=jax.ShapeDtypeStruct((bsz, length, branch), BF16),
        scratch_shapes=[pltpu.VMEM((n_chunks, ns, chunk), BF16),
                        pltpu.VMEM((length, ns), BF16),
                        pltpu.VMEM((length, gw), F32),
                        pltpu.VMEM((2, length, gw), BF16),
                        pltpu.VMEM((2, length, gw), BF16),
                        pltpu.VMEM((2 * n_chunks, 8, gw), F32),
                        pltpu.VMEM((2, ns, gw), F32),
                        ] + 2 * [
                        pltpu.VMEM((chunk, gw), F32),
                        pltpu.VMEM((2, chunk, gw), F32),
                        pltpu.VMEM((2, chunk, gw), F32),
                        pltpu.VMEM((2, chunk, LANES), F32),
                        pltpu.VMEM((2, LANES, chunk), F32),
                        pltpu.VMEM((chunk, chunk), F32)],
        compiler_params=_params("parallel", "parallel"), name="ssd",
    )(p3, bc3, bc3, p3, dt3, conv_w, conv_w, conv_w, conv_b, conv_b, conv_b,
      dt_bias_g, a_g, dsk)


def _ssd_group_lanes(v, groups):
    hpg = v.shape[1] // groups
    per = jnp.concatenate([v[0].reshape(groups, hpg), v[1].reshape(groups, hpg)], axis=1)
    per = jnp.pad(per, ((0, 0), (0, LANES - 2 * hpg)))
    return per.reshape(1, groups * LANES)


def kernel(x, norm_w, final_norm_w, ev_w_in, ev_w_out, hgrn_lb_logits, hgrn_norm_w, fnet_w, fnet_b,
           od_w_in, od_w_out, sconv_w, ssd_conv_w, ssd_conv_b, ssd_dt_bias, ssd_a_log, ssd_d, ssd_norm_w):
    bsz, length, d = x.shape
    t = bsz * length
    branch = d
    x2 = x.reshape(t, d)
    row = lambda v: v.astype(F32).reshape(1, -1)

    lower_bounds = jnp.cumsum(jax.nn.softmax(hgrn_lb_logits.astype(F32), axis=0), axis=0)
    p = _inproj(x2, row(norm_w[0]), ev_w_in[0].astype(BF16), n=ev_w_in.shape[2],
                tm=PROJ_ROW_TILE, tn=PROJ_COL_TILE)
    p3 = p.reshape(bsz, length, -1)
    a_out = _hgrn(p3, row(lower_bounds[0]), row(hgrn_norm_w[0]), col0=0, chunk=HGRN_CHUNK)
    b_out = _fnet(p3, fnet_w[0].astype(BF16), row(fnet_b[0]), u_col0=5 * branch, g_col0=6 * branch)
    x2 = _outproj(a_out.reshape(t, branch), b_out.reshape(t, branch), x2,
                  ev_w_out[0].astype(BF16), tm=OUTPROJ_ROW_TILE)

    heads = ssd_d.shape[1]
    groups = heads // SSD_GROUP_HEADS
    n_main = 5 * branch + branch + 2 * groups * SSD_STATE
    w_in = od_w_in[0]
    w_dt = w_in[:, n_main:].reshape(d, 2, groups, SSD_GROUP_HEADS)
    w_dt = jnp.transpose(w_dt, (0, 2, 1, 3)).reshape(d, 2 * heads)
    w_dt = jnp.pad(w_dt, ((0, 0), (0, LANES - 2 * heads)))
    w_main = _cast_transposed(w_in.T, n_main, tn=CAST_COL_TILE)
    n_bc = 2 * groups * SSD_STATE
    n_wide = n_main - n_bc
    p, h = _inproj(x2, row(norm_w[1]), w_main, n=n_wide, tm=PROJ_ROW_TILE, tn=PROJ_COL_TILE, keep_h=True)
    bc, dt = _proj_tail(h, w_main, w_dt.astype(BF16), col0=n_wide, n=n_bc, tm=PROJ_ROW_TILE)
    p3 = p.reshape(bsz, length, n_wide)
    c_out = _sconv(p3, sconv_w[0].astype(F32), tn=SCONV_COL_TILE)
    a_neg = -jnp.exp(ssd_a_log[0].astype(F32))
    d_out = _ssd(p3, bc.reshape(bsz, length, n_bc), dt.reshape(bsz, length, LANES),
                 ssd_conv_w[0].astype(F32), row(ssd_conv_b[0]),
                 _ssd_group_lanes(ssd_dt_bias[0].astype(F32), groups),
                 _ssd_group_lanes(a_neg, groups),
                 row(jnp.repeat(ssd_d[0].astype(F32), SSD_HEAD)),
                 x_col0=5 * branch, z_col0=4 * branch, chunk=SSD_CHUNK)
    out = _outproj(c_out.reshape(t, branch), d_out.reshape(t, branch), x2,
                   od_w_out[0].astype(BF16), row(ssd_norm_w[0]), row(final_norm_w), tm=OUTPROJ_ROW_TILE)
    return out.reshape(bsz, length, d)
```

```python
import functools

import jax
import jax.numpy as jnp
from jax import lax
from jax.experimental import pallas as pl
from jax.experimental.pallas import tpu as pltpu

F32 = jnp.float32
BF16 = jnp.bfloat16
EPS = 1e-6
LOG2E = 1.4426950408889634

VMEM_LIMIT_BYTES = 56 * 1024 * 1024
LANES = 128
BF16_SUBLANES = 16

PROJ_ROW_TILE = 1024
PROJ_COL_TILE = 2048
CAST_COL_TILE = 1024
OUTPROJ_ROW_TILE = 512
SCONV_COL_TILE = 512
SCONV_CHUNK = 128
NORM_ROWS = 128

HGRN_HEAD = 128
HGRN_CHUNK = 64
HGRN_BLOCK = 128
HGRN_SCAN_UNROLL = 16
FNET_TABLE_ROWS = 64
FNET_K_CHUNK = 1024
SSD_HEAD = 64
SSD_STATE = 128
SSD_GROUP_HEADS = 8
SSD_GROUP_W = SSD_HEAD * SSD_GROUP_HEADS
SSD_CHUNK = 128
SSD_SCAN_UNROLL = 8
SSD_PREPARE_UNROLL = 8


def _params(*sem):
    return pltpu.CompilerParams(dimension_semantics=sem, vmem_limit_bytes=VMEM_LIMIT_BYTES)


def _dot(a, b):
    return jnp.dot(a, b, preferred_element_type=F32)


def _dot_nt(a, b):
    return lax.dot_general(a, b, (((1,), (1,)), ((), ())), preferred_element_type=F32)


def _split2(x):
    hi = x.astype(BF16)
    lo = (x - hi.astype(F32)).astype(BF16)
    return hi, lo


def _sel_dot_rhs(sel_sel, x):
    return _dot(sel_sel, jnp.concatenate(_split2(x), axis=0))


def _silu(x):
    return x * jax.nn.sigmoid(x)


def _rms_scale(x):
    return lax.rsqrt(jnp.mean(x * x, axis=-1, keepdims=True) + EPS)


def _cast_t_kernel(wt_ref, o_ref):
    o_ref[...] = wt_ref[...].T.astype(o_ref.dtype)


def _cast_transposed(wt, n, *, tn):
    d = wt.shape[1]
    return pl.pallas_call(
        _cast_t_kernel, grid=(n // tn,),
        in_specs=[pl.BlockSpec((tn, d), lambda j: (j, 0))],
        out_specs=pl.BlockSpec((d, tn), lambda j: (0, j)),
        out_shape=jax.ShapeDtypeStruct((d, n), BF16),
        compiler_params=_params("parallel"), name="cast_t",
    )(wt)


def _norm_rows_to(h_ref, x_ref, nw_ref, row0, n_rows, rows_per_step):
    nw = nw_ref[...]
    for r in range(row0, row0 + n_rows, rows_per_step):
        x = x_ref[r:r + rows_per_step, :]
        h_ref[r:r + rows_per_step, :] = (x * _rms_scale(x) * nw).astype(BF16)


def _inproj_kernel(x_ref, nw_ref, w_ref, o_ref, h_ref):
    tm = x_ref.shape[0]
    first = pl.program_id(1) == 0

    @pl.when(first)
    def _():
        half = tm // 2
        for row0 in (0, half):
            _norm_rows_to(h_ref, x_ref, nw_ref, row0, half, min(NORM_ROWS, half))
            rows = slice(row0, row0 + half)
            o_ref[rows, :] = _dot(h_ref[rows, :], w_ref[...]).astype(o_ref.dtype)

    @pl.when(jnp.logical_not(first))
    def _():
        o_ref[...] = _dot(h_ref[...], w_ref[...]).astype(o_ref.dtype)


def _inproj(x2, nw, w, *, n, tm, tn, keep_h=False):
    t, d = x2.shape
    tm = min(tm, t)
    grid = (t // tm, n // tn)
    in_specs = [
        pl.BlockSpec((tm, d), lambda i, j: (i, 0)),
        pl.BlockSpec((1, d), lambda i, j: (0, 0)),
        pl.BlockSpec((d, tn), lambda i, j: (0, j)),
    ]
    out_main = pl.BlockSpec((tm, tn), lambda i, j: (i, j))
    if not keep_h:
        return pl.pallas_call(
            _inproj_kernel, grid=grid, in_specs=in_specs, out_specs=out_main,
            out_shape=jax.ShapeDtypeStruct((t, n), BF16), scratch_shapes=[pltpu.VMEM((tm, d), BF16)],
            compiler_params=_params("parallel", "arbitrary"), name="inproj",
        )(x2, nw, w)
    return pl.pallas_call(
        _inproj_kernel, grid=grid, in_specs=in_specs,
        out_specs=[out_main, pl.BlockSpec((tm, d), lambda i, j: (i, 0))],
        out_shape=[jax.ShapeDtypeStruct((t, n), BF16), jax.ShapeDtypeStruct((t, d), BF16)],
        compiler_params=_params("parallel", "arbitrary"), name="inproj_keep",
    )(x2, nw, w)


def _proj_tail_kernel(h_ref, w_ref, wdt_ref, o_ref, dt_ref):
    h = h_ref[...]
    o_ref[...] = _dot(h, w_ref[...]).astype(o_ref.dtype)
    dt_ref[...] = _dot(h, wdt_ref[...])


def _proj_tail(h, w, wdt, *, col0, n, tm):
    t, d = h.shape
    tm = min(tm, t)
    ndt = wdt.shape[1]
    rows = lambda width: pl.BlockSpec((tm, width), lambda i: (i, 0))
    return pl.pallas_call(
        _proj_tail_kernel, grid=(t // tm,),
        in_specs=[rows(d), pl.BlockSpec((d, n), lambda i: (0, col0 // n)), pl.BlockSpec((d, ndt), lambda i: (0, 0))],
        out_specs=[rows(n), rows(ndt)],
        out_shape=[jax.ShapeDtypeStruct((t, n), BF16), jax.ShapeDtypeStruct((t, ndt), F32)],
        compiler_params=_params("parallel"), name="proj_tail",
    )(h, w, wdt)


def _outproj_kernel(*refs, norm_b, final_norm):
    a_ref, b_ref, x_ref, w_ref = refs[:4]
    rest = list(refs[4:])
    o_ref = rest.pop()
    ka = a_ref.shape[1]
    bnw_ref = rest.pop(0) if norm_b else None
    fnw_ref = rest.pop(0) if final_norm else None
    b = b_ref[...]
    if norm_b:
        bf = b.astype(F32)
        b = (bf * _rms_scale(bf) * bnw_ref[...]).astype(BF16)
    acc = _dot(a_ref[...], w_ref[0:ka, :]) + _dot(b, w_ref[ka:, :]) + x_ref[...]
    if final_norm:
        acc = acc * _rms_scale(acc) * fnw_ref[...]
    o_ref[...] = acc


def _outproj(a, b, x2, w, b_norm_w=None, final_norm_w=None, *, tm):
    t, d = x2.shape
    ka, kb = a.shape[1], b.shape[1]
    tm = min(tm, t)
    row = lambda width: pl.BlockSpec((tm, width), lambda i: (i, 0))
    const = lambda shape: pl.BlockSpec(shape, lambda i: (0, 0))
    in_specs = [row(ka), row(kb), row(d), const((ka + kb, d))]
    args = [a, b, x2, w]
    if b_norm_w is not None:
        in_specs.append(const((1, kb)))
        args.append(b_norm_w)
    if final_norm_w is not None:
        in_specs.append(const((1, d)))
        args.append(final_norm_w)
    kern = functools.partial(_outproj_kernel, norm_b=b_norm_w is not None,
                             final_norm=final_norm_w is not None)
    return pl.pallas_call(
        kern, grid=(t // tm,), in_specs=in_specs, out_specs=row(d),
        out_shape=jax.ShapeDtypeStruct((t, d), F32),
        compiler_params=_params("parallel"), name="outproj",
    )(*args)


def _hgrn_kernel(q_ref, v_ref, zf_ref, zb_ref, g_ref, lb_ref, nw_ref, o_ref,
                 qin_s, kin_s, qdec_s, kendt_s, v_s, dec_s, acc_s, *, chunk, block, n_heads):
    length, width = acc_s.shape
    n_chunks = length // chunk
    n_blocks = length // block
    per_block = block // chunk
    mid = chunk // 2
    z_refs = (zf_ref, zb_ref)
    edge = (chunk - 1, 0)

    def prepare(blk, st):
        lb = lb_ref[...]
        one_m_lb = 1.0 - lb
        bri = lax.broadcasted_iota(jnp.int32, (block, block), 0)
        bci = lax.broadcasted_iota(jnp.int32, (block, block), 1)
        same_chunk = (bri // chunk) == (bci // chunk)
        r0 = pl.multiple_of(blk * block, block)
        rows = pl.ds(r0, block)
        q = q_ref[rows, :].astype(F32)
        v_s[st, rows, :] = v_ref[rows, :]
        for d in (0, 1):
            tri = jnp.where(same_chunk & ((bri <= bci) if d else (bri >= bci)), 1.0, 0.0).astype(BF16)
            z = z_refs[d][rows, :].astype(F32)
            sg = jax.nn.sigmoid(z)
            logf = jnp.log2(lb + one_m_lb * sg)
            key = one_m_lb * (1.0 - sg)
            b2 = _dot(tri, jnp.concatenate(_split2(logf), axis=1))
            b = b2[:, :width] + b2[:, width:]
            for k in range(per_block):
                sl = slice(k * chunk, (k + 1) * chunk)
                rk = pl.ds(r0 + k * chunk, chunk)
                bk = b[sl]
                b_mid = bk[mid:mid + 1, :]
                b_edge = bk[edge[d]:edge[d] + 1, :]
                q_in = q[sl] * jnp.exp2(bk - b_mid)
                k_in = key[sl] * jnp.exp2(b_mid - bk)
                qin_s[st, d, rk, :] = q_in.astype(BF16)
                kin_s[st, d, rk, :] = k_in.astype(BF16)
                qdec_s[st, d, rk, :] = (q_in * jnp.exp2(b_mid)).astype(BF16)
                idx = d * n_chunks + blk * per_block + k
                kendt_s[st, idx] = (k_in * jnp.exp2(b_edge - b_mid)).T.astype(BF16)
                dec_s[st, idx] = jnp.broadcast_to(jnp.exp2(b_edge), (width, width)).T

    def scan(c, states, st, finalize):
        ri = lax.broadcasted_iota(jnp.int32, (chunk, chunk), 0)
        ci = lax.broadcasted_iota(jnp.int32, (chunk, chunk), 1)
        new_states = []
        for d in (0, 1):
            cc = c if d == 0 else n_chunks - 1 - c
            rows = pl.ds(pl.multiple_of(cc * chunk, chunk), chunk)
            idx = d * n_chunks + cc
            v = v_s[st, rows, :]
            s = _dot_nt(qin_s[st, d, rows, :], kin_s[st, d, rows, :])
            s = jnp.where((ri <= ci) if d else (ri >= ci), s, 0.0).astype(BF16)
            o = _dot(jnp.concatenate([qdec_s[st, d, rows, :], s], axis=1),
                     jnp.concatenate([states[d].astype(BF16), v], axis=0))
            new_states.append(states[d] * dec_s[st, idx] + _dot(kendt_s[st, idx], v))
            if finalize:
                tot = acc_s[rows, :] + o
                g = g_ref[rows, :].astype(F32)
                o_ref[rows, :] = (tot * _rms_scale(tot) * nw_ref[...] * _silu(g)).astype(o_ref.dtype)
            else:
                acc_s[rows, :] = o
        return tuple(new_states)

    half = n_chunks // 2
    unroll = min(HGRN_SCAN_UNROLL, half)
    trips = half // unroll
    blocks_per_trip = n_blocks // (2 * trips)

    def run(prep_set, scan_set):
        def trip(i, states, *, finalize):
            if scan_set is not None:
                for u in range(unroll):
                    states = scan(i * unroll + u, states, scan_set, finalize)
            if prep_set is not None:
                for u in range(blocks_per_trip):
                    prepare(i * blocks_per_trip + u, prep_set)
            return states

        zero = jnp.zeros((width, width), F32)
        states = lax.fori_loop(0, trips, functools.partial(trip, finalize=False), (zero, zero))
        lax.fori_loop(trips, 2 * trips, functools.partial(trip, finalize=True), states)

    step = pl.program_id(0)
    inner = (step > 0) & (step < n_heads)
    pl.when(step == 0)(lambda: run(0, None))
    pl.when(inner & (step % 2 == 1))(lambda: run(1, 0))
    pl.when(inner & (step % 2 == 0))(lambda: run(0, 1))
    pl.when(step == n_heads)(lambda: run(None, (n_heads - 1) % 2))


def _hgrn(p3, lb, nw, *, col0, chunk):
    bsz, length, _ = p3.shape
    width = HGRN_HEAD
    branch = lb.shape[1]
    heads = branch // width
    n_heads = bsz * heads
    c0 = col0 // width
    n_chunks = length // chunk
    cur = lambda s: jnp.minimum(s, n_heads - 1)
    prev = lambda s: jnp.maximum(s - 1, 0)
    seq = lambda k, at: pl.BlockSpec((None, length, width),
                                     lambda s: (at(s) // heads, 0, c0 + k * heads + at(s) % heads))
    vec = lambda at: pl.BlockSpec((1, width), lambda s: (0, at(s) % heads))
    scaled = pltpu.VMEM((2, 2, length, width), BF16)
    return pl.pallas_call(
        functools.partial(_hgrn_kernel, chunk=chunk, block=min(HGRN_BLOCK, length), n_heads=n_heads),
        grid=(n_heads + 1,),
        in_specs=[seq(0, cur), seq(1, cur), seq(2, cur), seq(3, cur), seq(4, prev), vec(cur), vec(prev)],
        out_specs=pl.BlockSpec((None, length, width), lambda s: (prev(s) // heads, 0, prev(s) % heads)),
        out_shape=jax.ShapeDtypeStruct((bsz, length, branch), BF16),
        scratch_shapes=[scaled, scaled, scaled,
                        pltpu.VMEM((2, 2 * n_chunks, width, chunk), BF16),
                        pltpu.VMEM((2, length, width), BF16),
                        pltpu.VMEM((2, 2 * n_chunks, width, width), F32),
                        pltpu.VMEM((length, width), F32)],
        compiler_params=_params("arbitrary"), name="hgrn2",
    )(p3, p3, p3, p3, p3, lb, nw)


def _dft_cos_sin(n):
    k = lax.broadcasted_iota(jnp.int32, (n, n), 0) * lax.broadcasted_iota(jnp.int32, (n, n), 1)
    ang = (k % n).astype(F32) * (2.0 * jnp.pi / n)
    return jnp.cos(ang).astype(BF16), jnp.sin(ang).astype(BF16)


def _dft_rows(rows, n):
    k = rows[:, None] * lax.broadcasted_iota(jnp.int32, (rows.shape[0], n), 1)
    ang = (k % n).astype(F32) * (2.0 * jnp.pi / n)
    return jnp.cos(ang), jnp.sin(ang)


def _fnet_kernel(t1c_ref, t1s_ref, t2c_ref, t2s_ref, ccsc_ref, u_ref, fw_ref, fb_ref, g_ref, o_ref,
                 cs_s, *, scale):
    rb = t1c_ref.shape[0]
    length, gd = u_ref.shape

    @pl.when((pl.program_id(0) == 0) & (pl.program_id(1) == 0))
    def _():
        def gen(a, carry):
            rows = pl.ds(pl.multiple_of(a * rb, rb), rb)
            c2, s2 = t2c_ref[pl.ds(a, 1), :], t2s_ref[pl.ds(a, 1), :]
            c1, s1 = t1c_ref[...], t1s_ref[...]
            cs_s[rows, :length] = (c2 * c1 - s2 * s1).astype(BF16)
            cs_s[rows, length:] = (-(s2 * c1 + c2 * s1)).astype(BF16)
            return carry

        lax.fori_loop(0, t2c_ref.shape[0], gen, 0)

    z = _dot(u_ref[...], ccsc_ref[...])
    zz = jnp.concatenate([z[:, :gd], z[:, gd:]], axis=0).astype(BF16)
    kc = min(FNET_K_CHUNK, 2 * length)
    mixed = _dot(cs_s[:, :kc], zz[:kc])
    for k0 in range(kc, 2 * length, kc):
        mixed = mixed + _dot(cs_s[:, k0:k0 + kc], zz[k0:k0 + kc])
    y = _dot((mixed * scale).astype(BF16), fw_ref[...]) + fb_ref[...]
    o_ref[...] = (y * _silu(g_ref[...].astype(F32))).astype(o_ref.dtype)


def _fnet(p3, fw, fb, *, u_col0, g_col0):
    bsz, length, _ = p3.shape
    groups, gd, _ = fw.shape
    branch = groups * gd
    ccsc = jnp.concatenate(_dft_cos_sin(gd), axis=1)
    rb = FNET_TABLE_ROWS
    t1c, t1s = _dft_rows(jnp.arange(rb, dtype=jnp.int32), length)
    t2c, t2s = _dft_rows(jnp.arange(length // rb, dtype=jnp.int32) * rb, length)
    seq = lambda c0: pl.BlockSpec((None, length, gd), lambda b, g: (b, 0, c0 + g))
    table = lambda rows: pl.BlockSpec((rows, length), lambda b, g: (0, 0))
    return pl.pallas_call(
        functools.partial(_fnet_kernel, scale=float((length * gd) ** -0.5)),
        grid=(bsz, groups),
        in_specs=[table(rb), table(rb), table(length // rb), table(length // rb),
                  pl.BlockSpec((gd, 2 * gd), lambda b, g: (0, 0)),
                  seq(u_col0 // gd),
                  pl.BlockSpec((None, gd, gd), lambda b, g: (g, 0, 0)),
                  pl.BlockSpec((1, gd), lambda b, g: (0, g)),
                  seq(g_col0 // gd)],
        out_specs=seq(0),
        out_shape=jax.ShapeDtypeStruct((bsz, length, branch), BF16),
        scratch_shapes=[pltpu.VMEM((length, 2 * length), BF16)],
        compiler_params=_params("arbitrary", "arbitrary"), name="fnet",
    )(t1c, t1s, t2c, t2s, ccsc, p3, fw, fb, p3)


class _RowShift:
    def __init__(self, chunk, length):
        self.chunk, self.length, self.halo = chunk, length, BF16_SUBLANES
        self.ext = chunk + 2 * self.halo
        si = lax.broadcasted_iota(jnp.int32, (chunk, self.ext), 0)
        sj = lax.broadcasted_iota(jnp.int32, (chunk, self.ext), 1)
        self.mats = [jnp.concatenate([jnp.where(sj == si + (off - 1), 1.0, 0.0),
                                      jnp.where(sj == si + (off + 1), 1.0, 0.0)], axis=0).astype(BF16)
                     for off in (0, self.halo, 2 * self.halo)]

    def window(self, r0):
        start = pl.multiple_of(jnp.clip(r0 - self.halo, 0, self.length - self.ext), self.halo)
        off = r0 - start
        mat = jnp.where(off == self.halo, self.mats[1], jnp.where(off == 0, self.mats[0], self.mats[2]))
        return pl.ds(start, self.ext), mat


def _conv3_rows(load_rows, r0, n_rows, length, w):
    h = BF16_SUBLANES
    u = load_rows(r0, n_rows)
    prev_blk = load_rows(pl.multiple_of(jnp.maximum(r0 - h, 0), h), h)
    next_blk = load_rows(pl.multiple_of(jnp.minimum(r0 + n_rows, length - h), h), h)
    prev_row = jnp.where(r0 > 0, prev_blk[h - 1:h, :], 0.0)
    next_row = jnp.where(r0 + n_rows < length, next_blk[0:1, :], 0.0)
    ri = lax.broadcasted_iota(jnp.int32, u.shape, 0)
    up = jnp.where(ri == 0, prev_row, pltpu.roll(u, 1, 0))
    un = jnp.where(ri == n_rows - 1, next_row, pltpu.roll(u, n_rows - 1, 0))
    return up * w[0:1, :] + u * w[1:2, :] + un * w[2:3, :]


def _sconv_kernel(cin_ref, cb_ref, cc_ref, cg_ref, w_ref, o_ref, *, chunk):
    length, width = o_ref.shape

    def body(r, carry):
        r0 = pl.multiple_of(r * chunk, chunk)
        rows = pl.ds(r0, chunk)
        for j in range(width // LANES):
            cols = slice(j * LANES, (j + 1) * LANES)

            def load_u(start, size, cols=cols):
                rs = pl.ds(start, size)
                return cc_ref[rs, cols].astype(F32) * cin_ref[rs, cols].astype(F32)

            conv = _conv3_rows(load_u, r0, chunk, length, w_ref[:, cols])
            o_ref[rows, cols] = (cb_ref[rows, cols].astype(F32) * conv
                                 * _silu(cg_ref[rows, cols].astype(F32))).astype(o_ref.dtype)
        return carry

    lax.fori_loop(0, length // chunk, body, 0)


def _sconv(p3, w, *, tn):
    bsz, length, _ = p3.shape
    branch = w.shape[1]
    nb = branch // tn
    seq = lambda k: pl.BlockSpec((None, length, tn), lambda b, j, k=k: (b, 0, k * nb + j))
    return pl.pallas_call(
        functools.partial(_sconv_kernel, chunk=min(SCONV_CHUNK, length)),
        grid=(bsz, nb),
        in_specs=[seq(0), seq(1), seq(2), seq(3), pl.BlockSpec((3, tn), lambda b, j: (0, j))],
        out_specs=seq(0),
        out_shape=jax.ShapeDtypeStruct((bsz, length, branch), BF16),
        compiler_params=_params("parallel", "parallel"), name="sconv",
    )(p3, p3, p3, p3, w)


def _ssd_kernel(x_ref, bm_ref, cm_ref, z_ref, dt_ref,
                wx_ref, wb_ref, wc_ref, bx_ref, bb_ref, bc_ref,
                dtb_ref, a_ref, dsk_ref,
                o_ref,
                bmt_s, cm_s, yacc_s, ecol_s, xdte_s, alast_s, st_s,
                *staging, chunk):
    length, gw = yacc_s.shape
    n_chunks = length // chunk
    heads = SSD_GROUP_HEADS
    hd = gw // heads
    xs_t, ae_t, de_t, ac_t, at_t, cb_t = zip(staging[:6], staging[6:])
    shifter = _RowShift(chunk, length)
    edge = (chunk - 1, 0)

    ri = lax.broadcasted_iota(jnp.int32, (chunk, chunk), 0)
    ci = lax.broadcasted_iota(jnp.int32, (chunk, chunk), 1)
    masks = (ri >= ci, ri <= ci)
    tris = []
    for m in masks:
        tri = jnp.where(m, 1.0, 0.0).astype(BF16)
        tris.append(jnp.concatenate([tri, tri], axis=1))
    low = lax.broadcasted_iota(jnp.int32, (chunk, LANES), 1).astype(F32).astype(BF16) < hd
    sel_l =lax.broadcasted_iota(jnp.int32, (LANES, gw), 0)
    sel_c = lax.broadcasted_iota(jnp.int32, (LANES, gw), 1) // hd
    expands = []
    for d in (0, 1):
        e = jnp.where(sel_l == sel_c + d * heads, 1.0, 0.0).astype(BF16)
        expands.append(jnp.concatenate([e, e], axis=0))
    a2_row = a_ref[...] * LOG2E

    n_pairs = gw // LANES

    def front_steps(c, slot):
        env = {}

        def conv_silu(ref, w_ref, b_ref, cols):
            both = _dot(env["shift"], ref[env["erows"], cols])
            w = w_ref[:, cols]
            y = (both[:chunk] * w[0:1, :] + ref[env["rows"], cols].astype(F32) * w[1:2, :]
                 + both[chunk:] * w[2:3, :] + b_ref[:, cols])
            return _silu(y)

        def b_conv():
            r0 = pl.multiple_of(c * chunk, chunk)
            env["rows"] = pl.ds(r0, chunk)
            env["erows"], env["shift"] = shifter.window(r0)
            env["bmt"] = bmt_c = conv_silu(bm_ref, wb_ref, bb_ref, slice(0, LANES)).T.astype(BF16)
            bmt_s[c] = bmt_c

        def c_conv():
            cm_c = conv_silu(cm_ref, wc_ref, bc_ref, slice(0, LANES)).astype(BF16)
            cm_s[env["rows"], :] = cm_c
            cb_t[slot][...] = _dot(cm_c, env["bmt"])

        def decay(d):
            if d == 0:
                to_front = (LANES - pl.program_id(1) * 2 * heads) % LANES
                t = pltpu.roll(dt_ref[env["rows"], :], to_front, 1) + dtb_ref[...]
                env["dtc"] = jnp.maximum(t, 0.0) + jnp.log1p(jnp.exp(-jnp.abs(t)))
            a = _sel_dot_rhs(tris[d], env["dtc"] * a2_row)
            ac_t[slot][d] = a
            at_t[slot][d] = a.T

        def spread(d):
            ae_t[slot][d] = _dot(jnp.concatenate(_split2(ac_t[slot][d]), axis=1), expands[d])
            de_t[slot][d] = _dot(jnp.concatenate(_split2(env["dtc"]), axis=1), expands[d])

        def x_conv(s):
            wide = gw // 2
            wcols = slice(s * wide, (s + 1) * wide)
            xs_t[slot][:, wcols] = conv_silu(x_ref, wx_ref, bx_ref, wcols)

        part = functools.partial
        return [b_conv, c_conv, part(decay, 0), part(decay, 1), part(spread, 0), part(spread, 1),
                part(x_conv, 0), part(x_conv, 1)]

    def back_unit(c, slot, p, d, acc):
        rows = pl.ds(pl.multiple_of(c * chunk, chunk), chunk)
        slab = slice(p * LANES, (p + 1) * LANES)
        xs_p = xs_t[slot][:, slab]
        if d == 0:
            acc = xs_p * dsk_ref[:, slab]
        l0 = d * heads + 2 * p
        acum_p = ae_t[slot][d, :, slab]
        xdt = xs_p * de_t[slot][d, :, slab]
        a_edge = acum_p[edge[d]:edge[d] + 1, :]
        ecol_s[d, rows, slab] = jnp.exp2(acum_p).astype(BF16)
        xdte_s[d, rows, slab] = (xdt * jnp.exp2(a_edge - acum_p)).astype(BF16)
        alast_s[d * n_chunks + c, :, slab] = jnp.broadcast_to(jnp.exp2(a_edge), (8, LANES))
        ms = []
        for l in (l0, l0 + 1):
            diff = ac_t[slot][d, :, l:l + 1] - at_t[slot][d, l:l + 1, :]
            ms.append((cb_t[slot][...] * jnp.exp2(jnp.where(masks[d], diff, -1e30))).astype(BF16))
        xb = xdt.astype(BF16)
        zero = jnp.zeros_like(xb)
        block_diag = jnp.concatenate([jnp.where(low, xb, zero), jnp.where(low, zero, xb)], axis=0)
        acc = acc + _dot(jnp.concatenate(ms, axis=1), block_diag)
        if d == 1:
            yacc_s[rows, slab] = acc
        return acc

    def prepare_two(i, carry):
        c0 = 2 * i
        for c_back, s_back, c_front, s_front in ((c0, 0, c0 + 1, 1),
                                                 (c0 + 1, 1, jnp.minimum(c0 + 2, n_chunks - 1), 0)):
            steps = front_steps(c_front, s_front)
            acc = None
            for p in range(n_pairs):
                for d in (0, 1):
                    acc = back_unit(c_back, s_back, p, d, acc)
                    steps[2 * p + d]()
        return carry

    for step in front_steps(0, 0):
        step()
    lax.fori_loop(0, n_chunks // 2, prepare_two, 0, unroll=min(SSD_PREPARE_UNROLL, n_chunks // 2))

    st_s[...] = jnp.zeros_like(st_s)

    def scan(c, carry, *, finalize):
        for d in (0, 1):
            cc = c if d == 0 else n_chunks - 1 - c
            rows = pl.ds(pl.multiple_of(cc * chunk, chunk), chunk)
            state = st_s[d]
            y = _dot(cm_s[rows, :], state.astype(BF16)) * ecol_s[d, rows, :].astype(F32)
            st_s[d] = state * alast_s[d * n_chunks + cc][0:1, :] + _dot(bmt_s[cc], xdte_s[d, rows, :])
            if finalize:
                tot = yacc_s[rows, :] + y
                o_ref[rows, :] = (tot * _silu(z_ref[rows, :].astype(F32))).astype(o_ref.dtype)
            else:
                yacc_s[rows, :] = yacc_s[rows, :] + y
        return carry

    half = n_chunks // 2
    unroll = min(SSD_SCAN_UNROLL, half)
    lax.fori_loop(0, half, functools.partial(scan, finalize=False), 0, unroll=unroll)
    lax.fori_loop(half, n_chunks, functools.partial(scan, finalize=True), 0, unroll=unroll)


def _ssd(p3, bc3, dt3, conv_w, conv_b, dt_bias_g, a_g, dsk, *, x_col0, z_col0, chunk):
    bsz, length, _ = p3.shape
    gw, ns = SSD_GROUP_W, SSD_STATE
    branch = dsk.shape[1]
    groups = branch // gw
    xb, bb, cb = x_col0 // gw, 0, groups
    seq =lambda w, c0: pl.BlockSpec((None, length, w), lambda b, g: (b, 0, c0 + g))
    par = lambda rows, w, c0: pl.BlockSpec((rows, w), lambda b, g: (0, c0 + g))
    n_chunks = length // chunk
    return pl.pallas_call(
        functools.partial(_ssd_kernel, chunk=chunk),
        grid=(bsz, groups),
        in_specs=[seq(gw, xb), seq(ns, bb), seq(ns, cb), seq(gw, z_col0 // gw),
                  pl.BlockSpec((None, length, LANES), lambda b, g: (b, 0, 0)),
                  par(3, gw, 0), par(3, ns, branch // ns), par(3, ns, (branch + groups * ns) // ns),
                  par(1, gw, 0), par(1, ns, branch // ns), par(1, ns, (branch + groups * ns) // ns),
                  par(1, LANES, 0), par(1, LANES, 0),
                  par(1, gw, 0)],
        out_specs=seq(gw, 0),
        out_shape=jax.ShapeDtypeStruct((bsz, length, branch), BF16),
        scratch_shapes=[pltpu.VMEM((n_chunks, ns, chunk), BF16),
                        pltpu.VMEM((length, ns), BF16),
                        pltpu.VMEM((length, gw), F32),
                        pltpu.VMEM((2, length, gw), BF16),
                        pltpu.VMEM((2, length, gw), BF16),
                        pltpu.VMEM((2 * n_chunks, 8, gw), F32),
                        pltpu.VMEM((2, ns, gw), F32),
                        ] + 2 * [
                        pltpu.VMEM((chunk, gw), F32),
                        pltpu.VMEM((2, chunk, gw), F32),
                        pltpu.VMEM((2, chunk, gw), F32),
                        pltpu.VMEM((2, chunk, LANES), F32),
                        pltpu.VMEM((2, LANES, chunk), F32),
                        pltpu.VMEM((chunk, chunk), F32)],
        compiler_params=_params("parallel", "parallel"), name="ssd",
    )(p3, bc3, bc3, p3, dt3, conv_w, conv_w, conv_w, conv_b, conv_b, conv_b,
      dt_bias_g, a_g, dsk)


def _ssd_group_lanes(v, groups):
    hpg = v.shape[1] // groups
    per = jnp.concatenate([v[0].reshape(groups, hpg), v[1].reshape(groups, hpg)], axis=1)
    per = jnp.pad(per, ((0, 0), (0, LANES - 2 * hpg)))
    return per.reshape(1, groups * LANES)


def kernel(x, norm_w, final_norm_w, ev_w_in, ev_w_out, hgrn_lb_logits, hgrn_norm_w, fnet_w, fnet_b,
           od_w_in, od_w_out, sconv_w, ssd_conv_w, ssd_conv_b, ssd_dt_bias, ssd_a_log, ssd_d, ssd_norm_w):
    bsz, length, d = x.shape
    t = bsz * length
    branch = d
    x2 = x.reshape(t, d)
    row = lambda v: v.astype(F32).reshape(1, -1)

    lower_bounds = jnp.cumsum(jax.nn.softmax(hgrn_lb_logits.astype(F32), axis=0), axis=0)
    p = _inproj(x2, row(norm_w[0]), ev_w_in[0].astype(BF16), n=ev_w_in.shape[2],
                tm=PROJ_ROW_TILE, tn=PROJ_COL_TILE)
    p3 = p.reshape(bsz, length, -1)
    a_out = _hgrn(p3, row(lower_bounds[0]), row(hgrn_norm_w[0]), col0=0, chunk=HGRN_CHUNK)
    b_out = _fnet(p3, fnet_w[0].astype(BF16), row(fnet_b[0]), u_col0=5 * branch, g_col0=6 * branch)
    x2 = _outproj(a_out.reshape(t, branch), b_out.reshape(t, branch), x2,
                  ev_w_out[0].astype(BF16), tm=OUTPROJ_ROW_TILE)

    heads = ssd_d.shape[1]
    groups = heads // SSD_GROUP_HEADS
    n_main = 5 * branch + branch + 2 * groups * SSD_STATE
    w_in = od_w_in[0]
    w_dt = w_in[:, n_main:].reshape(d, 2, groups, SSD_GROUP_HEADS)
    w_dt = jnp.transpose(w_dt, (0, 2, 1, 3)).reshape(d, 2 * heads)
    w_dt = jnp.pad(w_dt, ((0, 0), (0, LANES - 2 * heads)))
    w_main = _cast_transposed(w_in.T, n_main, tn=CAST_COL_TILE)
    n_bc = 2 * groups * SSD_STATE
    n_wide = n_main - n_bc
    p, h = _inproj(x2, row(norm_w[1]), w_main, n=n_wide, tm=PROJ_ROW_TILE, tn=PROJ_COL_TILE, keep_h=True)
    bc, dt = _proj_tail(h, w_main, w_dt.astype(BF16), col0=n_wide, n=n_bc, tm=PROJ_ROW_TILE)
    p3 = p.reshape(bsz, length, n_wide)
    c_out = _sconv(p3, sconv_w[0].astype(F32), tn=SCONV_COL_TILE)
    a_neg = -jnp.exp(ssd_a_log[0].astype(F32))
    d_out = _ssd(p3, bc.reshape(bsz, length, n_bc), dt.reshape(bsz, length, LANES),
                 ssd_conv_w[0].astype(F32), row(ssd_conv_b[0]),
                 _ssd_group_lanes(ssd_dt_bias[0].astype(F32), groups),
                 _ssd_group_lanes(a_neg, groups),
                 row(jnp.repeat(ssd_d[0].astype(F32), SSD_HEAD)),
                 x_col0=5 * branch, z_col0=4 * branch, chunk=SSD_CHUNK)
    out = _outproj(c_out.reshape(t, branch), d_out.reshape(t, branch), x2,
                   od_w_out[0].astype(BF16), row(ssd_norm_w[0]), row(final_norm_w), tm=OUTPROJ_ROW_TILE)
    return out.reshape(bsz, length, d)
```
